```python
import math
import jax
import jax.numpy as jnp
from jax import lax
import numpy as np


D_MODEL = 2048
BATCH = 4
SEQ = 2048
DEPTH = 4
DEC_BATCH = 8
DEC_SEQ = 1
PAST_LEN = 16384
PAGE_SIZE = 128

HEAD_DIM = 128
NSA_HEADS = 8
NSA_KV_HEADS = 2
NSA_GROUP = NSA_HEADS // NSA_KV_HEADS
NSA_BLOCK = 64
NSA_TOPN = 16
WINDOW = 512
SB_HEADS = 4
DIFF_HEADS = 4
DIFF_DIM = HEAD_DIM // 2
MIX_WIDTH = (NSA_HEADS + SB_HEADS + DIFF_HEADS) * HEAD_DIM
D_FF = 4 * D_MODEL
ROPE_THETA = 500000.0
ROT_FRACTION = 4
Q_BLOCK = 128
LN_EPS = 1e-5
DEEPNORM_ALPHA = (2 * DEPTH) ** 0.25
DEEPNORM_BETA = (8 * DEPTH) ** -0.25
NEG_INF = -1e30
POS_INF = 1e30

kernel_name = 'hybrid_nsa_stickbreak_diffattn_decoder_step'


def _column_layout():
    kvw = NSA_KV_HEADS * HEAD_DIM
    return [
        ('nsa_q', NSA_HEADS * HEAD_DIM),
        ('nsa_cmp_k', kvw), ('nsa_cmp_v', kvw),
        ('nsa_slc_k', kvw), ('nsa_slc_v', kvw),
        ('nsa_win_k', kvw), ('nsa_win_v', kvw),
        ('nsa_gate', 3 * NSA_HEADS),
        ('sb_q', SB_HEADS * HEAD_DIM), ('sb_k', SB_HEADS * HEAD_DIM), ('sb_v', SB_HEADS * HEAD_DIM),
        ('diff_q', DIFF_HEADS * HEAD_DIM), ('diff_k', DIFF_HEADS * HEAD_DIM), ('diff_v', DIFF_HEADS * HEAD_DIM),
        ('merge_gate', MIX_WIDTH),
    ]


def layer_norm(x, g, b):
    xf = x.astype(jnp.float32)
    mu = jnp.mean(xf, axis=-1, keepdims=True)
    var = jnp.mean(jnp.square(xf - mu), axis=-1, keepdims=True)
    return ((xf - mu) * lax.rsqrt(var + LN_EPS) * g + b).astype(x.dtype)


def rope(x, pos, rot_dim):
    half = rot_dim // 2
    inv_freq = ROPE_THETA ** (-jnp.arange(half, dtype=jnp.float32) / half)
    ang = pos.astype(jnp.float32)[:, None] * inv_freq[None, :]
    cos = jnp.cos(ang)[None, :, None, :]
    sin = jnp.sin(ang)[None, :, None, :]
    xf = x.astype(jnp.float32)
    x1 = xf[..., :half]
    x2 = xf[..., half:rot_dim]
    out = jnp.concatenate([x1 * cos - x2 * sin, x2 * cos + x1 * sin, xf[..., rot_dim:]], axis=-1)
    return out.astype(x.dtype)


def masked_softmax(s, mask):
    p = jax.nn.softmax(jnp.where(mask, s, NEG_INF), axis=-1)
    return jnp.where(jnp.any(mask, axis=-1, keepdims=True), p, 0.0)


def to_blocks(rows):
    b, l, h, d = rows.shape
    return rows.reshape(b, l // NSA_BLOCK, NSA_BLOCK, h, d).transpose(0, 3, 1, 2, 4)


def nsa_compress(rows, pos_emb, w1, w2):
    b, l, h, d = rows.shape
    nb = l // NSA_BLOCK
    blocks = rows.reshape(b, nb, NSA_BLOCK, h, d) + pos_emb[None, None, :, None, :]
    flat = blocks.transpose(0, 1, 3, 2, 4).reshape(b, nb, h, NSA_BLOCK * d)
    return jax.nn.gelu(flat @ w1) @ w2


def compress_kv(k_rows, v_rows, p):
    kc = nsa_compress(k_rows, p['cmp_pos'][0], p['cmp_w1'][0], p['cmp_w2'][0])
    vc = nsa_compress(v_rows, p['cmp_pos'][1], p['cmp_w1'][1], p['cmp_w2'][1])
    return kc, vc


def mixer_inputs(x, w_in, pos):
    b, t, _ = x.shape
    layout = _column_layout()
    points = [int(c) for c in np.cumsum([s for _, s in layout])[:-1]]
    u = jnp.einsum('btd,de->bte', x, w_in)
    parts = dict(zip([n for n, _ in layout], jnp.split(u, points, axis=-1)))
    rot = HEAD_DIM // ROT_FRACTION

    def heads(a, h):
        return a.reshape(b, t, h, HEAD_DIM)

    def grp(a):
        return a.reshape(b, t, NSA_KV_HEADS, NSA_GROUP, a.shape[-1])

    def diff_split(a):
        a = rope(a.reshape(b, t, DIFF_HEADS * 2, DIFF_DIM), pos, DIFF_DIM // ROT_FRACTION)
        return a.reshape(b, t, DIFF_HEADS, 2, DIFF_DIM)

    q_raw = heads(parts['nsa_q'], NSA_HEADS)
    return {
        'q_rot': grp(rope(q_raw, pos, rot)),
        'q_raw': grp(q_raw),
        'cmp_k': heads(parts['nsa_cmp_k'], NSA_KV_HEADS),
        'cmp_v': heads(parts['nsa_cmp_v'], NSA_KV_HEADS),
        'slc_k': rope(heads(parts['nsa_slc_k'], NSA_KV_HEADS), pos, rot),
        'slc_v': heads(parts['nsa_slc_v'], NSA_KV_HEADS),
        'win_k': rope(heads(parts['nsa_win_k'], NSA_KV_HEADS), pos, rot),
        'win_v': heads(parts['nsa_win_v'], NSA_KV_HEADS),
        'nsa_gate': parts['nsa_gate'].reshape(b, t, NSA_KV_HEADS, NSA_GROUP, 3),
        'sb_q': heads(parts['sb_q'], SB_HEADS),
        'sb_k': heads(parts['sb_k'], SB_HEADS),
        'sb_v': heads(parts['sb_v'], SB_HEADS),
        'diff_q': diff_split(parts['diff_q']),
        'diff_k': diff_split(parts['diff_k']),
        'diff_v': heads(parts['diff_v'], DIFF_HEADS),
        'merge_gate': parts['merge_gate'],
    }


def nsa_query_block(q_rot, q_raw, gate, pos_q, kc, vc, ks_blk, vs_blk, kw, vw, kpos_w):
    b, tq = q_rot.shape[:2]
    nb = kc.shape[1]
    scale = HEAD_DIM ** -0.5
    blk = jnp.arange(nb, dtype=jnp.int32)
    s_c = jnp.einsum('bqhgd,bnhd->bqhgn', q_raw, kc).astype(jnp.float32) * scale
    vis_c = ((blk[None, :] + 1) * NSA_BLOCK - 1) <= pos_q[:, None]
    p_c = masked_softmax(s_c, vis_c[None, :, None, None, :])
    o_c = jnp.einsum('bqhgn,bnhd->bqhgd', p_c.astype(vc.dtype), vc)
    imp = jnp.sum(p_c, axis=3)
    cur = (pos_q // NSA_BLOCK)[None, :, None, None]
    bb = blk[None, None, None, :]
    imp = jnp.where(bb == cur, POS_INF, jnp.where(bb < cur, imp, NEG_INF))
    n_sel = min(NSA_TOPN, nb)
    _, idx = lax.top_k(imp, n_sel)
    idx_h = idx.transpose(0, 2, 1, 3).reshape(b, NSA_KV_HEADS, tq * n_sel)
    bi = jnp.arange(b)[:, None, None]
    hi = jnp.arange(NSA_KV_HEADS)[None, :, None]
    kg = ks_blk[bi, hi, idx_h].reshape(b, NSA_KV_HEADS, tq, n_sel * NSA_BLOCK, HEAD_DIM)
    vg = vs_blk[bi, hi, idx_h].reshape(b, NSA_KV_HEADS, tq, n_sel * NSA_BLOCK, HEAD_DIM)
    kpos_s = (idx_h.reshape(b, NSA_KV_HEADS, tq, n_sel, 1) * NSA_BLOCK
              + jnp.arange(NSA_BLOCK, dtype=jnp.int32)).reshape(b, NSA_KV_HEADS, tq, n_sel * NSA_BLOCK)
    vis_s = kpos_s <= pos_q[None, None, :, None]
    s_s = jnp.einsum('bqhgd,bhqkd->bhqgk', q_rot, kg).astype(jnp.float32) * scale
    p_s = masked_softmax(s_s, vis_s[:, :, :, None, :])
    o_s = jnp.einsum('bhqgk,bhqkd->bqhgd', p_s.astype(vg.dtype), vg)
    dpos = pos_q[:, None] - kpos_w[None, :]
    vis_w = (kpos_w[None, :] >= 0) & (dpos >= 0) & (dpos < WINDOW)
    s_w = jnp.einsum('bqhgd,bkhd->bqhgk', q_rot, kw).astype(jnp.float32) * scale
    p_w = masked_softmax(s_w, vis_w[None, :, None, None, :])
    o_w = jnp.einsum('bqhgk,bkhd->bqhgd', p_w.astype(vw.dtype), vw)
    g = jax.nn.sigmoid(gate.astype(jnp.float32)).astype(o_c.dtype)
    out = g[..., 0:1] * o_c + g[..., 1:2] * o_s + g[..., 2:3] * o_w
    return out.reshape(b, tq, NSA_HEADS * HEAD_DIM)


def stick_breaking_block(q, pos_q, k, v, kpos):
    b, tq = q.shape[:2]
    z = jnp.einsum('bqhd,bkhd->bhqk', q, k).astype(jnp.float32) * HEAD_DIM ** -0.5
    valid = (kpos[None, :] < pos_q[:, None])[None, None]
    log_keep = jnp.where(valid, -jax.nn.softplus(z), 0.0)
    later = lax.cumsum(log_keep, axis=3, reverse=True) - log_keep
    a = jnp.where(valid, jnp.exp(jax.nn.log_sigmoid(z) + later), 0.0)
    o = jnp.einsum('bhqk,bkhd->bqhd', a.astype(v.dtype), v)
    return o.reshape(b, tq, SB_HEADS * HEAD_DIM)


def diff_block(q, pos_q, k, v, kpos, lam, lam_init, norm_g):
    b, tq = q.shape[:2]
    s = jnp.einsum('bqhcd,bkhcd->bhcqk', q, k).astype(jnp.float32) * DIFF_DIM ** -0.5
    vis = (kpos[None, :] <= pos_q[:, None])[None, None, None]
    p = jax.nn.softmax(jnp.where(vis, s, NEG_INF), axis=-1)
    w = p[:, :, 0] - lam * p[:, :, 1]
    o = jnp.einsum('bhqk,bkhd->bqhd', w.astype(v.dtype), v).astype(jnp.float32)
    o = o * lax.rsqrt(jnp.mean(jnp.square(o), axis=-1, keepdims=True) + LN_EPS) * norm_g * (1.0 - lam_init)
    return o.reshape(b, tq, DIFF_HEADS * HEAD_DIM).astype(v.dtype)


def merge_and_ffn(x, o_nsa, o_sb, o_diff, merge_gate, p):
    mixed = jnp.concatenate([o_nsa, o_sb, o_diff], axis=-1) * jax.nn.sigmoid(merge_gate)
    x = layer_norm(DEEPNORM_ALPHA * x + mixed @ p['w_o'], p['ln1_g'], p['ln1_b'])
    h = jnp.square(jax.nn.relu(x @ p['w_up']))
    return layer_norm(DEEPNORM_ALPHA * x + h @ p['w_down'], p['ln2_g'], p['ln2_b'])


def sweep_query_blocks(block_fn, b, t):
    qb = Q_BLOCK if t % Q_BLOCK == 0 else t
    out = lax.map(lambda i: block_fn(i * qb, qb), jnp.arange(t // qb, dtype=jnp.int32))
    return jnp.moveaxis(out, 0, 1).reshape(b, t, out.shape[-1])


def prompt_layer(x, p):
    b, t, _ = x.shape
    pos = jnp.arange(t, dtype=jnp.int32)
    m = mixer_inputs(x, p['w_in'], pos)
    kc, vc = compress_kv(m['cmp_k'], m['cmp_v'], p)
    ks_blk, vs_blk = to_blocks(m['slc_k']), to_blocks(m['slc_v'])
    pad = ((0, 0), (WINDOW, 0), (0, 0), (0, 0))
    kw_pad, vw_pad = jnp.pad(m['win_k'], pad), jnp.pad(m['win_v'], pad)

    def nsa_fn(start, qb):
        def sl(a):
            return lax.dynamic_slice_in_dim(a, start, qb, axis=1)

        def band(a):
            return lax.dynamic_slice_in_dim(a, start, WINDOW + qb, axis=1)

        return nsa_query_block(sl(m['q_rot']), sl(m['q_raw']), sl(m['nsa_gate']),
                               start + jnp.arange(qb, dtype=jnp.int32), kc, vc, ks_blk, vs_blk,
                               band(kw_pad), band(vw_pad),
                               start - WINDOW + jnp.arange(WINDOW + qb, dtype=jnp.int32))

    def sb_fn(start, qb):
        return stick_breaking_block(lax.dynamic_slice_in_dim(m['sb_q'], start, qb, axis=1),
                                    start + jnp.arange(qb, dtype=jnp.int32), m['sb_k'], m['sb_v'], pos)

    def diff_fn(start, qb):
        return diff_block(lax.dynamic_slice_in_dim(m['diff_q'], start, qb, axis=1),
                          start + jnp.arange(qb, dtype=jnp.int32), m['diff_k'], m['diff_v'], pos,
                          p['lam'], p['lam_init'], p['diff_norm_g'])

    o_nsa = sweep_query_blocks(nsa_fn, b, t)
    o_sb = sweep_query_blocks(sb_fn, b, t)
    o_diff = sweep_query_blocks(diff_fn, b, t)
    y = merge_and_ffn(x, o_nsa, o_sb, o_diff, m['merge_gate'], p)
    win_rows = min(WINDOW, t)
    new_cmp = jnp.stack([m['cmp_k'], m['cmp_v']], axis=2)
    new_slc = jnp.stack([m['slc_k'], m['slc_v']], axis=2)
    new_win = jnp.stack([m['win_k'], m['win_v']], axis=2)[:, t - win_rows:]
    new_sb = jnp.stack([m['sb_k'], m['sb_v']], axis=2)
    new_diff = jnp.stack([m['diff_k'].reshape(b, t, DIFF_HEADS, HEAD_DIM), m['diff_v']], axis=2)
    return y, new_cmp, new_slc, new_win, new_sb, new_diff


def sample_layer(x, cmp_cache, slc_cache, win_state, sb_cache, diff_cache, page_table, p):
    b, tq, _ = x.shape
    past = page_table.shape[1] * PAGE_SIZE
    total = past + tq
    padded = -(-total // NSA_BLOCK) * NSA_BLOCK
    pos_q = past + jnp.arange(tq, dtype=jnp.int32)
    kpos = jnp.arange(total, dtype=jnp.int32)
    m = mixer_inputs(x, p['w_in'], pos_q)

    def full_rows(cache, new_k, new_v):
        rows = cache[page_table].reshape((b, past) + cache.shape[2:])
        return (jnp.concatenate([rows[:, :, 0], new_k], axis=1),
                jnp.concatenate([rows[:, :, 1], new_v], axis=1))

    pad = ((0, 0), (0, padded - total), (0, 0), (0, 0))
    cmp_k, cmp_v = full_rows(cmp_cache, m['cmp_k'], m['cmp_v'])
    kc, vc = compress_kv(jnp.pad(cmp_k, pad), jnp.pad(cmp_v, pad), p)
    slc_k, slc_v = full_rows(slc_cache, m['slc_k'], m['slc_v'])
    ks_blk, vs_blk = to_blocks(jnp.pad(slc_k, pad)), to_blocks(jnp.pad(slc_v, pad))
    wb = win_state.shape[1]
    kw = jnp.concatenate([win_state[:, :, 0], m['win_k']], axis=1)
    vw = jnp.concatenate([win_state[:, :, 1], m['win_v']], axis=1)
    kpos_w = past - wb + jnp.arange(wb + tq, dtype=jnp.int32)
    o_nsa = nsa_query_block(m['q_rot'], m['q_raw'], m['nsa_gate'], pos_q, kc, vc, ks_blk, vs_blk, kw, vw, kpos_w)
    sb_k, sb_v = full_rows(sb_cache, m['sb_k'], m['sb_v'])
    o_sb = stick_breaking_block(m['sb_q'], pos_q, sb_k, sb_v, kpos)
    new_dk = m['diff_k'].reshape(b, tq, DIFF_HEADS, HEAD_DIM)
    dk, dv = full_rows(diff_cache, new_dk, m['diff_v'])
    o_diff = diff_block(m['diff_q'], pos_q, dk.reshape(b, total, DIFF_HEADS, 2, DIFF_DIM), dv, kpos,
                        p['lam'], p['lam_init'], p['diff_norm_g'])
    y = merge_and_ffn(x, o_nsa, o_sb, o_diff, m['merge_gate'], p)
    win_rows = min(WINDOW, total)
    new_win = jnp.concatenate([win_state, jnp.stack([m['win_k'], m['win_v']], axis=2)], axis=1)[:, wb + tq - win_rows:]
    new_cmp = jnp.stack([m['cmp_k'], m['cmp_v']], axis=2)
    new_slc = jnp.stack([m['slc_k'], m['slc_v']], axis=2)
    new_sb = jnp.stack([m['sb_k'], m['sb_v']], axis=2)
    new_diff = jnp.stack([new_dk, m['diff_v']], axis=2)
    return y, new_cmp, new_slc, new_win, new_sb, new_diff


def setup_inputs(seed: int = 0) -> dict:
    key = jax.random.key(seed)
    ks = jax.random.split(key, 24)
    f32 = jnp.float32
    n_pages = PAST_LEN // PAGE_SIZE
    n_pool = (DEC_BATCH * n_pages * 5) // 4
    win_buf = min(WINDOW, PAST_LEN)

    def nrm(k, shape, s):
        return jax.random.normal(k, shape, f32) * s

    layout = _column_layout()
    in_width = sum(s for _, s in layout)
    col_scale = np.concatenate([np.full((s,), DEEPNORM_BETA if n.endswith('_v') else 1.0, dtype=np.float32)
                                for n, s in layout])
    page_table = jax.random.permutation(ks[7], n_pool)[: DEC_BATCH * n_pages].reshape(DEC_BATCH, n_pages).astype(jnp.int32)
    return {
        'x_prompt': nrm(ks[0], (BATCH, SEQ, D_MODEL), 1.0),
        'x_sample': nrm(ks[1], (DEC_BATCH, DEC_SEQ, D_MODEL), 1.0),
        'cache_nsa_cmp_kv': nrm(ks[2], (DEPTH, n_pool, PAGE_SIZE, 2, NSA_KV_HEADS, HEAD_DIM), 1.0),
        'cache_nsa_slc_kv': nrm(ks[3], (DEPTH, n_pool, PAGE_SIZE, 2, NSA_KV_HEADS, HEAD_DIM), 1.0),
        'state_nsa_win_kv': nrm(ks[4], (DEPTH, DEC_BATCH, win_buf, 2, NSA_KV_HEADS, HEAD_DIM), 1.0),
        'cache_sb_kv': nrm(ks[5], (DEPTH, n_pool, PAGE_SIZE, 2, SB_HEADS, HEAD_DIM), 1.0),
        'cache_diff_kv': nrm(ks[6], (DEPTH, n_pool, PAGE_SIZE, 2, DIFF_HEADS, HEAD_DIM), 1.0),
        'page_table': page_table,
        'w_in': nrm(ks[8], (DEPTH, D_MODEL, in_width), D_MODEL ** -0.5) * jnp.asarray(col_scale),
        'w_o': nrm(ks[9], (DEPTH, MIX_WIDTH, D_MODEL), DEEPNORM_BETA * MIX_WIDTH ** -0.5),
        'ln1_g': 1.0 + nrm(ks[10], (DEPTH, D_MODEL), 0.02),
        'ln1_b': nrm(ks[11], (DEPTH, D_MODEL), 0.02),
        'ln2_g': 1.0 + nrm(ks[12], (DEPTH, D_MODEL), 0.02),
        'ln2_b': nrm(ks[13], (DEPTH, D_MODEL), 0.02),
        'w_up': nrm(ks[14], (DEPTH, D_MODEL, D_FF), DEEPNORM_BETA * D_MODEL ** -0.5),
        'w_down': nrm(ks[15], (DEPTH, D_FF, D_MODEL), DEEPNORM_BETA * D_FF ** -0.5),
        'nsa_cmp_pos': nrm(ks[16], (DEPTH, 2, NSA_BLOCK, HEAD_DIM), 0.1),
        'nsa_cmp_w1': nrm(ks[17], (DEPTH, 2, NSA_BLOCK * HEAD_DIM, HEAD_DIM), (NSA_BLOCK * HEAD_DIM) ** -0.5),
        'nsa_cmp_w2': nrm(ks[18], (DEPTH, 2, HEAD_DIM, HEAD_DIM), HEAD_DIM ** -0.5),
        'diff_lambda': nrm(ks[19], (DEPTH, 4, DIFF_DIM), 0.1),
        'diff_norm_g': 1.0 + nrm(ks[20], (DEPTH, HEAD_DIM), 0.02),
    }


def reference(x_prompt, x_sample, cache_nsa_cmp_kv, cache_nsa_slc_kv, state_nsa_win_kv, cache_sb_kv,
              cache_diff_kv, page_table, w_in, w_o, ln1_g, ln1_b, ln2_g, ln2_b, w_up, w_down,
              nsa_cmp_pos, nsa_cmp_w1, nsa_cmp_w2, diff_lambda, diff_norm_g):
    xp, xs = x_prompt, x_sample
    st_p = [[] for _ in range(5)]
    st_s = [[] for _ in range(5)]
    for l in range(DEPTH):
        lam_init = 0.8 - 0.6 * math.exp(-0.3 * l)
        lv = diff_lambda[l].astype(jnp.float32)
        lam = jnp.exp(jnp.sum(lv[0] * lv[1])) - jnp.exp(jnp.sum(lv[2] * lv[3])) + lam_init
        p = {
            'w_in': w_in[l], 'w_o': w_o[l],
            'ln1_g': ln1_g[l], 'ln1_b': ln1_b[l], 'ln2_g': ln2_g[l], 'ln2_b': ln2_b[l],
            'w_up': w_up[l], 'w_down': w_down[l],
            'cmp_pos': nsa_cmp_pos[l], 'cmp_w1': nsa_cmp_w1[l], 'cmp_w2': nsa_cmp_w2[l],
            'lam': lam, 'lam_init': lam_init, 'diff_norm_g': diff_norm_g[l],
        }
        xp, *new_p = prompt_layer(xp, p)
        xs, *new_s = sample_layer(xs, cache_nsa_cmp_kv[l], cache_nsa_slc_kv[l], state_nsa_win_kv[l],
                                  cache_sb_kv[l], cache_diff_kv[l], page_table, p)
        for i in range(5):
            st_p[i].append(new_p[i])
            st_s[i].append(new_s[i])
    return (xp, xs,
            jnp.stack(st_p[0]), jnp.stack(st_s[0]),
            jnp.stack(st_p[1]), jnp.stack(st_s[1]),
            jnp.stack(st_p[2]), jnp.stack(st_s[2]),
            jnp.stack(st_p[3]), jnp.stack(st_s[3]),
            jnp.stack(st_p[4]), jnp.stack(st_s[4]))
```

```python
import functools
import math

import numpy as np
import jax
import jax.numpy as jnp
from jax import lax
from jax.experimental import pallas as pl
from jax.experimental.pallas import tpu as pltpu

HEAD_DIM = 128
NSA_HEADS = 8
NSA_KV_HEADS = 2
NSA_GROUP = NSA_HEADS // NSA_KV_HEADS
NSA_BLOCK = 64
NSA_TOPN = 16
WINDOW = 512
SB_HEADS = 4
DIFF_HEADS = 4
DIFF_DIM = HEAD_DIM // 2
ROPE_THETA = 500000.0
ROT_FRACTION = 4
LN_EPS = 1e-5
NEG_INF = -1e30

F32 = jnp.float32
MXU_DTYPE = jnp.bfloat16

LANES = 128
COL_TILE = 256
VMEM_LIMIT_BYTES = 56 * 2**20

KV_WIDTH = NSA_KV_HEADS * HEAD_DIM
Q_TILES = NSA_HEADS * HEAD_DIM // COL_TILE
T_CMP = Q_TILES
T_SLC = T_CMP + 2
T_WIN = T_SLC + 2
T_SBQ = T_WIN + 2
T_SBKV = T_SBQ + SB_HEADS * HEAD_DIM // COL_TILE
T_DQ = T_SBKV + 2 * SB_HEADS * HEAD_DIM // COL_TILE
T_DKV = T_DQ + DIFF_HEADS * HEAD_DIM // COL_TILE
T_MG = T_DKV + 2 * DIFF_HEADS * HEAD_DIM // COL_TILE
MIX_WIDTH = (NSA_HEADS + SB_HEADS + DIFF_HEADS) * HEAD_DIM
T_NG = T_MG + MIX_WIDTH // COL_TILE
N_TILES = T_NG + 1


def _dot(a, b):
    return jnp.dot(a, b, preferred_element_type=F32)


def _dot_nt(a, b):
    return lax.dot_general(a, b, (((1,), (1,)), ((), ())), preferred_element_type=F32)


def _split(x):
    hi = x.astype(MXU_DTYPE)
    lo = (x - hi.astype(F32)).astype(MXU_DTYPE)
    return hi, lo


def _dot2(x, w):
    hi, lo = _split(x)
    return _dot(hi, w) + _dot(lo, w)


def _params(*sem):
    return pltpu.CompilerParams(dimension_semantics=sem, vmem_limit_bytes=VMEM_LIMIT_BYTES)


def _softplus(z):
    return jnp.maximum(z, 0.0) + jnp.log(1.0 + jnp.exp(-jnp.abs(z)))


def _layer_norm(h, g, b):
    mu = jnp.mean(h, axis=-1, keepdims=True)
    d = h - mu
    var = jnp.mean(d * d, axis=-1, keepdims=True)
    return d * lax.rsqrt(var + LN_EPS) * g + b


def _rope_tables(pos, width, rot_dim):
    half = rot_dim // 2
    inv_freq = ROPE_THETA ** (-jnp.arange(half, dtype=F32) / half)
    ang = pos.astype(F32)[:, None] * inv_freq[None, :]
    cos, sin = jnp.cos(ang), jnp.sin(ang)
    n = pos.shape[0]
    rest = width - 2 * half
    c = jnp.concatenate([cos, cos, jnp.ones((n, rest), F32)], axis=1)
    s_up = jnp.concatenate([jnp.zeros((n, half), F32), sin, jnp.zeros((n, rest), F32)], axis=1)
    s_dn = jnp.concatenate([-sin, jnp.zeros((n, half + rest), F32)], axis=1)
    rep = LANES // width
    return tuple(jnp.tile(a, (1, rep)) for a in (c, s_up, s_dn))


def _proj_kernel(x_ref, w_ref, c_ref, su_ref, sd_ref, cd_ref, sud_ref, sdd_ref,
                 qraw_ref, qrot_ref, cmp_ref, slc_ref, win_ref, sbq_ref, sbkv_ref,
                 dq_ref, dkv_ref, mg_ref, ng_ref):
    n = pl.program_id(1)
    acc = _dot(x_ref[...], w_ref[...])

    def rope(a, c, su, sd, shift):
        def one(h):
            return (h * c[...] + pltpu.roll(h, shift, 1) * su[...]
                    + pltpu.roll(h, LANES - shift, 1) * sd[...])
        return jnp.concatenate([one(a[:, :LANES]), one(a[:, LANES:])], axis=1)

    def rope_head(a):
        return rope(a, c_ref, su_ref, sd_ref, HEAD_DIM // ROT_FRACTION // 2)

    def rope_diff(a):
        return rope(a, cd_ref, sud_ref, sdd_ref, DIFF_DIM // ROT_FRACTION // 2)

    def between(lo, hi):
        return jnp.logical_and(n >= lo, n < hi)

    @pl.when(n < T_CMP)
    def _():
        qraw_ref[...] = acc.astype(qraw_ref.dtype)
        qrot_ref[...] = rope_head(acc).astype(qrot_ref.dtype)

    @pl.when(between(T_CMP, T_SLC))
    def _():
        cmp_ref[...] = acc

    @pl.when(n == T_SLC)
    def _():
        slc_ref[...] = rope_head(acc)

    @pl.when(n == T_SLC + 1)
    def _():
        slc_ref[...] = acc

    @pl.when(n == T_WIN)
    def _():
        win_ref[...] = rope_head(acc)

    @pl.when(n == T_WIN + 1)
    def _():
        win_ref[...] = acc

    @pl.when(between(T_SBQ, T_SBKV))
    def _():
        sbq_ref[...] = acc.astype(sbq_ref.dtype)

    @pl.when(between(T_SBKV, T_DQ))
    def _():
        sbkv_ref[...] = acc

    @pl.when(between(T_DQ, T_DKV))
    def _():
        dq_ref[...] = rope_diff(acc).astype(dq_ref.dtype)

    @pl.when(between(T_DKV, T_DKV + 2))
    def _():
        dkv_ref[...] = rope_diff(acc)

    @pl.when(between(T_DKV + 2, T_MG))
    def _():
        dkv_ref[...] = acc

    @pl.when(between(T_MG, T_NG))
    def _():
        mg_ref[...] = jax.nn.sigmoid(acc)

    @pl.when(n == T_NG)
    def _():
        ng_ref[...] = jax.nn.sigmoid(acc)


def _project(xb, w_p, tabs, tm, rows_per_seq):
    m, d = xb.shape
    seq_tiles = rows_per_seq // tm

    def omap(start, cnt):
        return lambda i, n: (i, jnp.clip(n - start, 0, cnt - 1))

    def out(width, dtype, start, cnt):
        return (jax.ShapeDtypeStruct((m, width), dtype), pl.BlockSpec((tm, COL_TILE), omap(start, cnt)))

    outs = [
        out(NSA_HEADS * HEAD_DIM, MXU_DTYPE, 0, Q_TILES),
        out(NSA_HEADS * HEAD_DIM, MXU_DTYPE, 0, Q_TILES),
        out(2 * KV_WIDTH, F32, T_CMP, 2),
        out(2 * KV_WIDTH, F32, T_SLC, 2),
        out(2 * KV_WIDTH, F32, T_WIN, 2),
        out(SB_HEADS * HEAD_DIM, MXU_DTYPE, T_SBQ, T_SBKV - T_SBQ),
        out(2 * SB_HEADS * HEAD_DIM, F32, T_SBKV, T_DQ - T_SBKV),
        out(DIFF_HEADS * HEAD_DIM, MXU_DTYPE, T_DQ, T_DKV - T_DQ),
        out(2 * DIFF_HEADS * HEAD_DIM, F32, T_DKV, T_MG - T_DKV),
        out(MIX_WIDTH, F32, T_MG, T_NG - T_MG),
        out(COL_TILE, F32, T_NG, 1),
    ]
    tab_spec = pl.BlockSpec((tm, LANES), lambda i, n: (i % seq_tiles, 0))
    return pl.pallas_call(
        _proj_kernel,
        grid=(m // tm, N_TILES),
        in_specs=[pl.BlockSpec((tm, d), lambda i, n: (i, 0)),
                  pl.BlockSpec((d, COL_TILE), lambda i, n: (0, n))] + [tab_spec] * 6,
        out_specs=[o[1] for o in outs],
        out_shape=[o[0] for o in outs],
        compiler_params=_params("arbitrary", "arbitrary"),
        name="proj_in",
    )(xb, w_p, *tabs)


def _permute_w_in(w):
    d = w.shape[0]
    g0 = Q_TILES * COL_TILE + 6 * KV_WIDTH
    ng = 3 * NSA_HEADS
    per = ng // NSA_KV_HEADS
    gate = w[:, g0:g0 + ng]
    pad = jnp.zeros((d, LANES - per), w.dtype)
    gate_tile = jnp.concatenate([gate[:, :per], pad, gate[:, per:], pad], axis=1)
    return jnp.concatenate([w[:, :g0], w[:, g0 + ng:], gate_tile], axis=1).astype(MXU_DTYPE)


R_CHUNK = 16
SLABS = 2 * NSA_KV_HEADS


def _slab_rows(ref, first, count, stride, slabs):
    return jnp.concatenate([ref[pl.ds(first + c, count, stride=stride), :] for c in slabs], axis=1)


def _gelu(x):
    return 0.5 * x * (1.0 + jnp.tanh(math.sqrt(2.0 / math.pi) * (x + 0.044715 * (x * x * x))))


def _compress_kernel(pt_ref, pos_ref, w1_ref, w2_ref, cache_ref, out_ref, buf, sem, acc_ref,
                     *, layer, npg, bpp, n_steps):
    b = pl.program_id(0)
    j = pl.program_id(1)
    n_chunks = NSA_BLOCK // R_CHUNK
    step = b * n_chunks + j
    slot = step % 2
    nblk = npg * bpp

    def copy(bb, jj, sl, i, h):
        page = pt_ref[bb * npg + i]
        src = cache_ref.at[layer, page, pl.ds((h * NSA_BLOCK + jj * R_CHUNK) * SLABS, R_CHUNK * SLABS), :]
        dst = buf.at[sl, pl.ds((i * bpp + h) * R_CHUNK * SLABS, R_CHUNK * SLABS), :]
        return pltpu.make_async_copy(src, dst, sem.at[sl])

    def for_all(bb, jj, sl, start):
        def body(i, carry):
            for h in range(bpp):
                cp = copy(bb, jj, sl, i, h)
                if start:
                    cp.start()
                else:
                    cp.wait()
            return carry
        lax.fori_loop(0, npg, body, 0)

    @pl.when(step == 0)
    def _():
        for_all(b, j, slot, True)

    @pl.when(step + 1 < n_steps)
    def _():
        nxt = step + 1
        for_all(nxt // n_chunks, nxt % n_chunks, 1 - slot, True)

    for_all(b, j, slot, False)

    @pl.when(j == 0)
    def _():
        acc_ref[...] = jnp.zeros_like(acc_ref)

    half = 2 * HEAD_DIM
    a_k = jnp.zeros((nblk, half), F32)
    a_v = jnp.zeros((nblk, half), F32)
    for r in range(R_CHUNK):
        xr = _slab_rows(buf.at[slot], r * SLABS, nblk, R_CHUNK * SLABS, range(SLABS))
        xr = xr + pos_ref[pl.ds(j * R_CHUNK + r, 1), :]
        xb = xr.astype(MXU_DTYPE)
        a_k = a_k + _dot(xb[:, :half], w1_ref[r, 0])
        a_v = a_v + _dot(xb[:, half:], w1_ref[r, 1])
    acc_ref[:, :half] += a_k
    acc_ref[:, half:] += a_v

    @pl.when(j == n_chunks - 1)
    def _():
        h = _gelu(acc_ref[...]).astype(MXU_DTYPE)
        out_ref[0] = _dot(h, w2_ref[...])


def _compress(cache, layer, table, posc, w1bd, w2bd):
    nb, npg = table.shape
    rpp = cache.shape[2] // SLABS
    bpp = rpp // NSA_BLOCK
    nblk = npg * bpp
    n_chunks = NSA_BLOCK // R_CHUNK
    width = 2 * KV_WIDTH
    kern = functools.partial(_compress_kernel, layer=layer, npg=npg, bpp=bpp, n_steps=nb * n_chunks)
    return pl.pallas_call(
        kern,
        grid_spec=pltpu.PrefetchScalarGridSpec(
            num_scalar_prefetch=1,
            grid=(nb, n_chunks),
            in_specs=[pl.BlockSpec((NSA_BLOCK, width), lambda b, j, pt: (0, 0)),
                      pl.BlockSpec((R_CHUNK,) + w1bd.shape[1:], lambda b, j, pt: (j, 0, 0, 0)),
                      pl.BlockSpec(w2bd.shape, lambda b, j, pt: (0, 0)),
                      pl.BlockSpec(memory_space=pl.ANY)],
            out_specs=pl.BlockSpec((1, nblk, width), lambda b, j, pt: (b, 0, 0)),
            scratch_shapes=[pltpu.VMEM((2, nblk * R_CHUNK * SLABS, HEAD_DIM), F32),
                            pltpu.SemaphoreType.DMA((2,)),
                            pltpu.VMEM((nblk, width), F32)]),
        out_shape=jax.ShapeDtypeStruct((nb, nblk, width), F32),
        compiler_params=_params("arbitrary", "arbitrary"),
        name="nsa_compress",
    )(table.reshape(-1), posc, w1bd, w2bd, cache)


def _compress_weights(pos, w1, w2):
    posc = jnp.concatenate([pos[0]] * NSA_KV_HEADS + [pos[1]] * NSA_KV_HEADS, axis=1)
    w1r = w1.reshape(2, NSA_BLOCK, HEAD_DIM, HEAD_DIM)
    z = jnp.zeros_like(w1r)
    w1bd = jnp.concatenate([jnp.concatenate([w1r, z], axis=3), jnp.concatenate([z, w1r], axis=3)], axis=2)
    w1bd = jnp.transpose(w1bd, (1, 0, 2, 3)).astype(MXU_DTYPE)
    blocks = [w2[0]] * NSA_KV_HEADS + [w2[1]] * NSA_KV_HEADS
    n = len(blocks)
    w2bd = jnp.concatenate(
        [jnp.concatenate([blk if c == r else jnp.zeros_like(blk) for c in range(n)], axis=1)
         for r, blk in enumerate(blocks)], axis=0).astype(MXU_DTYPE)
    return posc, w1bd, w2bd


NSA_TQ = 128
NSA_KB = 256


def _online_softmax_step(x, mask, v, m_ref, l_ref, acc_ref):
    xm = jnp.where(mask, x, NEG_INF)
    m_old = m_ref[...]
    m_new = jnp.maximum(m_old, jnp.max(xm, axis=1, keepdims=True))
    alpha = jnp.exp(m_old - m_new)
    p = jnp.where(mask, jnp.exp(xm - m_new), 0.0)
    l_ref[...] = alpha * l_ref[...] + jnp.sum(p, axis=1, keepdims=True)
    acc_ref[...] = alpha * acc_ref[...] + _dot(p.astype(MXU_DTYPE), v)
    m_ref[...] = m_new


def _nsa_prompt_kernel(qraw_ref, qrot_ref, kc_ref, vc_ref, sk_ref, sv_ref, wk_ref, wv_ref, ng_ref,
                       eblk_ref, out_ref, m_ref, l_ref, acc_ref, *, seq, nb):
    qi = pl.program_id(2)
    tq = NSA_TQ
    g4 = NSA_GROUP
    rows = g4 * tq
    t0 = qi * tq
    scale = HEAD_DIM ** -0.5
    n_sel = min(NSA_TOPN, nb)

    def stack(ref):
        return jnp.concatenate([ref[:, g * HEAD_DIM:(g + 1) * HEAD_DIM] for g in range(g4)], axis=0)

    def tile_rows(a):
        return jnp.concatenate([a] * g4, axis=0)

    qraw = stack(qraw_ref)
    qrot = stack(qrot_ref)
    t_col = t0 + lax.broadcasted_iota(jnp.int32, (tq, 1), 0)
    t_rows = tile_rows(t_col)

    kc = kc_ref[...].astype(MXU_DTYPE)
    vc = vc_ref[...].astype(MXU_DTYPE)
    s_c = _dot_nt(qraw, kc) * scale
    blk = lax.broadcasted_iota(jnp.int32, (rows, LANES), 1)
    vis_c = ((blk + 1) * NSA_BLOCK - 1) <= t_rows
    m_c = jnp.max(jnp.where(vis_c, s_c, NEG_INF), axis=1, keepdims=True)
    e_c = jnp.where(vis_c, jnp.exp(jnp.where(vis_c, s_c, NEG_INF) - m_c), 0.0)
    den = jnp.sum(e_c, axis=1, keepdims=True)
    p_c = e_c / jnp.where(den > 0.0, den, 1.0)
    o_c = _dot(p_c.astype(MXU_DTYPE), vc)

    imp = p_c[0:tq]
    for g in range(1, g4):
        imp = imp + p_c[g * tq:(g + 1) * tq]
    imp_t = imp.T[0:nb, :]
    t_lane = t0 + lax.broadcasted_iota(jnp.int32, (1, tq), 1)
    cur = t_lane // NSA_BLOCK
    n_idx = lax.broadcasted_iota(jnp.int32, (nb, tq), 0)
    rank = jnp.zeros((nb, tq), jnp.int32)
    for mm in range(nb):
        row = imp_t[mm:mm + 1, :]
        ahead = jnp.where(row > imp_t, 1, jnp.where(jnp.logical_and(row == imp_t, mm < n_idx), 1, 0))
        rank = rank + jnp.where(mm < cur, ahead, 0)
    sel_t = jnp.where(n_idx == cur, 1.0, jnp.where(n_idx < cur, jnp.where(rank < n_sel - 1, 1.0, 0.0), 0.0))
    sel_t = jnp.concatenate([sel_t, jnp.zeros((LANES - nb, tq), F32)], axis=0)
    sel = sel_t.T.astype(MXU_DTYPE)

    m_ref[...] = jnp.full_like(m_ref, NEG_INF)
    l_ref[...] = jnp.zeros_like(l_ref)
    acc_ref[...] = jnp.zeros_like(acc_ref)
    n_chunks = (t0 + tq - 1) // NSA_KB + 1

    def slc_body(kj, carry):
        k0 = pl.multiple_of(kj * NSA_KB, NSA_KB)
        kb = sk_ref[pl.ds(k0, NSA_KB), :].astype(MXU_DTYPE)
        vb = sv_ref[pl.ds(k0, NSA_KB), :].astype(MXU_DTYPE)
        s = _dot_nt(qrot, kb) * scale
        selm = tile_rows(_dot(sel, eblk_ref[kj]))
        kpos = k0 + lax.broadcasted_iota(jnp.int32, (rows, NSA_KB), 1)
        mask = jnp.logical_and(selm > 0.5, kpos <= t_rows)
        _online_softmax_step(s, mask, vb, m_ref, l_ref, acc_ref)
        return carry

    lax.fori_loop(0, n_chunks, slc_body, 0)
    o_s = acc_ref[...] / l_ref[...]

    band = min(WINDOW + tq, seq)
    w0 = pl.multiple_of(jnp.clip(t0 + tq - band, 0, seq - band), tq)
    kw = wk_ref[pl.ds(w0, band), :].astype(MXU_DTYPE)
    vw = wv_ref[pl.ds(w0, band), :].astype(MXU_DTYPE)
    s_w = _dot_nt(qrot, kw) * scale
    dpos = t_rows - (w0 + lax.broadcasted_iota(jnp.int32, (rows, band), 1))
    vis_w = jnp.logical_and(dpos >= 0, dpos < WINDOW)
    xw = jnp.where(vis_w, s_w, NEG_INF)
    e_w = jnp.where(vis_w, jnp.exp(xw - jnp.max(xw, axis=1, keepdims=True)), 0.0)
    o_w = _dot(e_w.astype(MXU_DTYPE), vw) / jnp.sum(e_w, axis=1, keepdims=True)

    gates = ng_ref[...]
    for g in range(g4):
        r = slice(g * tq, (g + 1) * tq)
        out_ref[:, g * HEAD_DIM:(g + 1) * HEAD_DIM] = (
            gates[:, 3 * g:3 * g + 1] * o_c[r] + gates[:, 3 * g + 1:3 * g + 2] * o_s[r]
            + gates[:, 3 * g + 2:3 * g + 3] * o_w[r])


def _nsa_prompt(qraw, qrot, kcv, slc, win, ng, batch, seq):
    m = qraw.shape[0]
    nb = seq // NSA_BLOCK
    assert nb <= LANES and seq % NSA_KB == 0 and seq % NSA_TQ == 0
    nq = seq // NSA_TQ
    nkc = seq // NSA_KB
    kcv_p = jnp.pad(kcv.reshape(batch, nb, 2 * KV_WIDTH), ((0, 0), (0, LANES - nb), (0, 0)))
    kpos = np.arange(seq).reshape(nkc, 1, NSA_KB) // NSA_BLOCK
    eblk = jnp.asarray(kpos == np.arange(LANES).reshape(1, LANES, 1), dtype=MXU_DTYPE)
    gw = NSA_GROUP * HEAD_DIM
    rows = NSA_GROUP * NSA_TQ
    q_spec = pl.BlockSpec((NSA_TQ, gw), lambda b, h, q: (b * nq + q, h))

    def kv_spec(off):
        return pl.BlockSpec((seq, HEAD_DIM), lambda b, h, q: (b, off + h))

    return pl.pallas_call(
        functools.partial(_nsa_prompt_kernel, seq=seq, nb=nb),
        grid=(batch, NSA_KV_HEADS, nq),
        in_specs=[q_spec, q_spec,
                  pl.BlockSpec((None, LANES, HEAD_DIM), lambda b, h, q: (b, 0, h)),
                  pl.BlockSpec((None, LANES, HEAD_DIM), lambda b, h, q: (b, 0, NSA_KV_HEADS + h)),
                  kv_spec(0), kv_spec(NSA_KV_HEADS), kv_spec(0), kv_spec(NSA_KV_HEADS),
                  pl.BlockSpec((NSA_TQ, LANES), lambda b, h, q: (b * nq + q, h)),
                  pl.BlockSpec(eblk.shape, lambda b, h, q: (0, 0, 0))],
        out_specs=pl.BlockSpec((NSA_TQ, gw), lambda b, h, q: (b * nq + q, h)),
        out_shape=jax.ShapeDtypeStruct((m, NSA_HEADS * HEAD_DIM), F32),
        scratch_shapes=[pltpu.VMEM((rows, 1), F32), pltpu.VMEM((rows, 1), F32),
                        pltpu.VMEM((rows, HEAD_DIM), F32)],
        compiler_params=_params("arbitrary", "arbitrary", "arbitrary"),
        name="nsa_prompt",
    )(qraw, qrot, kcv_p, kcv_p, slc, slc, win, win, ng, eblk)


SB_TQ = 256


def _suffix_matrix(kb):
    j = np.arange(kb).reshape(kb, 1)
    s = np.arange(kb + LANES).reshape(1, kb + LANES)
    return jnp.asarray(np.logical_or(j > s, s >= kb), dtype=MXU_DTYPE)


def _sb_prompt_kernel(q_ref, k_ref, v_ref, lw_ref, out_ref, tail_ref, acc_ref):
    qi = pl.program_id(2)
    tq = SB_TQ
    t0 = qi * tq
    scale = HEAD_DIM ** -0.5
    q = q_ref[...]
    t_rows = t0 + lax.broadcasted_iota(jnp.int32, (tq, tq), 0)
    tail_ref[...] = jnp.zeros_like(tail_ref)
    acc_ref[...] = jnp.zeros_like(acc_ref)

    def body(kk, carry):
        kj = qi - kk
        k0 = pl.multiple_of(kj * tq, tq)
        kb = k_ref[pl.ds(k0, tq), :].astype(MXU_DTYPE)
        vb = v_ref[pl.ds(k0, tq), :].astype(MXU_DTYPE)
        z = _dot_nt(q, kb) * scale
        valid = (k0 + lax.broadcasted_iota(jnp.int32, (tq, tq), 1)) < t_rows
        sp = _softplus(z)
        log_keep = jnp.where(valid, -sp, 0.0)
        sums = _dot2(log_keep, lw_ref[...])
        tail = tail_ref[...]
        later = sums[:, :tq] + jnp.concatenate([tail] * (tq // LANES), axis=1)
        a = jnp.where(valid, jnp.exp(z - sp + later), 0.0)
        acc_ref[...] += _dot(a.astype(MXU_DTYPE), vb)
        tail_ref[...] = tail + sums[:, tq:]
        return carry

    lax.fori_loop(0, qi + 1, body, 0)
    out_ref[...] = acc_ref[...]


def _sb_prompt(sbq, sbkv, batch, seq):
    m = sbq.shape[0]
    assert seq % SB_TQ == 0
    nq = seq // SB_TQ
    lw = _suffix_matrix(SB_TQ)
    return pl.pallas_call(
        _sb_prompt_kernel,
        grid=(batch, SB_HEADS, nq),
        in_specs=[pl.BlockSpec((SB_TQ, HEAD_DIM), lambda b, h, q: (b * nq + q, h)),
                  pl.BlockSpec((seq, HEAD_DIM), lambda b, h, q: (b, h)),
                  pl.BlockSpec((seq, HEAD_DIM), lambda b, h, q: (b, SB_HEADS + h)),
                  pl.BlockSpec(lw.shape, lambda b, h, q: (0, 0))],
        out_specs=pl.BlockSpec((SB_TQ, HEAD_DIM), lambda b, h, q: (b * nq + q, h)),
        out_shape=jax.ShapeDtypeStruct((m, SB_HEADS * HEAD_DIM), F32),
        scratch_shapes=[pltpu.VMEM((SB_TQ, LANES), F32), pltpu.VMEM((SB_TQ, HEAD_DIM), F32)],
        compiler_params=_params("arbitrary", "arbitrary", "arbitrary"),
        name="sb_prompt",
    )(sbq, sbkv, sbkv, lw)


DIFF_TQ = 256


def _diff_lambda(lam_ref, lam_init):
    lv = lam_ref[...]
    s1 = jnp.sum(lv[0:1] * lv[1:2], axis=1, keepdims=True)
    s2 = jnp.sum(lv[2:3] * lv[3:4], axis=1, keepdims=True)
    return jnp.exp(s1) - jnp.exp(s2) + lam_init


def _diff_finish(o0, o1, lam, g, lam_init):
    o = o0 - lam * o1
    o = o * lax.rsqrt(jnp.mean(o * o, axis=-1, keepdims=True) + LN_EPS)
    return o * g * (1.0 - lam_init)


def _diff_prompt_kernel(q_ref, k_ref, v_ref, lam_ref, g_ref, out_ref, m_ref, l_ref, acc_ref, *, lam_init):
    qi = pl.program_id(2)
    tq = DIFF_TQ
    t0 = qi * tq
    scale = DIFF_DIM ** -0.5
    q = q_ref[...]
    lane = lax.broadcasted_iota(jnp.int32, (tq, HEAD_DIM), 1)
    zero = jnp.zeros_like(q)
    qq = jnp.concatenate([jnp.where(lane < DIFF_DIM, q, zero), jnp.where(lane >= DIFF_DIM, q, zero)], axis=0)
    t_col = t0 + lax.broadcasted_iota(jnp.int32, (tq, tq), 0)
    t_rows = jnp.concatenate([t_col, t_col], axis=0)
    m_ref[...] = jnp.full_like(m_ref, NEG_INF)
    l_ref[...] = jnp.zeros_like(l_ref)
    acc_ref[...] = jnp.zeros_like(acc_ref)

    def body(kj, carry):
        k0 = pl.multiple_of(kj * tq, tq)
        kb = k_ref[pl.ds(k0, tq), :].astype(MXU_DTYPE)
        vb = v_ref[pl.ds(k0, tq), :].astype(MXU_DTYPE)
        s = _dot_nt(qq, kb) * scale
        mask = (k0 + lax.broadcasted_iota(jnp.int32, (2 * tq, tq), 1)) <= t_rows
        _online_softmax_step(s, mask, vb, m_ref, l_ref, acc_ref)
        return carry

    lax.fori_loop(0, qi + 1, body, 0)
    o = acc_ref[...] / l_ref[...]
    lam = _diff_lambda(lam_ref, lam_init)
    out_ref[...] = _diff_finish(o[:tq], o[tq:], lam, g_ref[...], lam_init)


def _diff_prompt(dq, dkv, lam_rows, norm_g, lam_init, batch, seq):
    m = dq.shape[0]
    assert seq % DIFF_TQ == 0
    nq = seq // DIFF_TQ
    return pl.pallas_call(
        functools.partial(_diff_prompt_kernel, lam_init=lam_init),
        grid=(batch, DIFF_HEADS, nq),
        in_specs=[pl.BlockSpec((DIFF_TQ, HEAD_DIM), lambda b, h, q: (b * nq + q, h)),
                  pl.BlockSpec((seq, HEAD_DIM), lambda b, h, q: (b, h)),
                  pl.BlockSpec((seq, HEAD_DIM), lambda b, h, q: (b, DIFF_HEADS + h)),
                  pl.BlockSpec(lam_rows.shape, lambda b, h, q: (0, 0)),
                  pl.BlockSpec((1, HEAD_DIM), lambda b, h, q: (0, 0))],
        out_specs=pl.BlockSpec((DIFF_TQ, HEAD_DIM), lambda b, h, q: (b * nq + q, h)),
        out_shape=jax.ShapeDtypeStruct((m, DIFF_HEADS * HEAD_DIM), F32),
        scratch_shapes=[pltpu.VMEM((2 * DIFF_TQ, 1), F32), pltpu.VMEM((2 * DIFF_TQ, 1), F32),
                        pltpu.VMEM((2 * DIFF_TQ, HEAD_DIM), F32)],
        compiler_params=_params("arbitrary", "arbitrary", "arbitrary"),
        name="diff_prompt",
    )(dq, dkv, dkv, lam_rows, norm_g.reshape(1, HEAD_DIM))


def _merge_kernel(on_ref, os_ref, od_ref, mg_ref, x_ref, wo_ref, g_ref, b_ref, y_ref, yb_ref, *, alpha):
    n1 = on_ref.shape[1]
    n2 = n1 + os_ref.shape[1]
    r = _dot((on_ref[...] * mg_ref[:, :n1]).astype(MXU_DTYPE), wo_ref[:n1, :])
    r = r + _dot((os_ref[...] * mg_ref[:, n1:n2]).astype(MXU_DTYPE), wo_ref[n1:n2, :])
    r = r + _dot((od_ref[...] * mg_ref[:, n2:]).astype(MXU_DTYPE), wo_ref[n2:, :])
    y = _layer_norm(alpha * x_ref[...] + r, g_ref[...], b_ref[...])
    y_ref[...] = y
    yb_ref[...] = y.astype(yb_ref.dtype)


def _merge(o_nsa, o_sb, o_diff, mg, x, wo, g, b, alpha, tm):
    m, d = x.shape

    def row(width):
        return pl.BlockSpec((tm, width), lambda i: (i, 0))

    def full(a):
        return pl.BlockSpec(a.shape, lambda i: (0, 0))

    g2, b2 = g.reshape(1, d), b.reshape(1, d)
    return pl.pallas_call(
        functools.partial(_merge_kernel, alpha=alpha),
        grid=(m // tm,),
        in_specs=[row(o_nsa.shape[1]), row(o_sb.shape[1]), row(o_diff.shape[1]), row(mg.shape[1]), row(d),
                  full(wo), full(g2), full(b2)],
        out_specs=[row(d), row(d)],
        out_shape=[jax.ShapeDtypeStruct((m, d), F32), jax.ShapeDtypeStruct((m, d), MXU_DTYPE)],
        compiler_params=_params("arbitrary"),
        name="merge_out_proj",
    )(o_nsa, o_sb, o_diff, mg, x, wo, g2, b2)


FFN_TF = 512


def _ffn_kernel(x_ref, xb_ref, wu_ref, wd_ref, g_ref, b_ref, y_ref, yb_ref, acc_ref, *, alpha):
    j = pl.program_id(1)

    @pl.when(j == 0)
    def _():
        acc_ref[...] = jnp.zeros_like(acc_ref)

    h = jnp.maximum(_dot(xb_ref[...], wu_ref[...]), 0.0)
    acc_ref[...] += _dot((h * h).astype(MXU_DTYPE), wd_ref[...])

    @pl.when(j == pl.num_programs(1) - 1)
    def _():
        y = _layer_norm(alpha * x_ref[...] + acc_ref[...], g_ref[...], b_ref[...])
        y_ref[...] = y
        yb_ref[...] = y.astype(yb_ref.dtype)


def _ffn(x, xb, wu, wd, g, b, alpha, tm):
    m, d = x.shape
    dff = wu.shape[1]
    g2, b2 = g.reshape(1, d), b.reshape(1, d)
    return pl.pallas_call(
        functools.partial(_ffn_kernel, alpha=alpha),
        grid=(m // tm, dff // FFN_TF),
        in_specs=[pl.BlockSpec((tm, d), lambda i, j: (i, 0)),
                  pl.BlockSpec((tm, d), lambda i, j: (i, 0)),
                  pl.BlockSpec((d, FFN_TF), lambda i, j: (0, j)),
                  pl.BlockSpec((FFN_TF, d), lambda i, j: (j, 0)),
                  pl.BlockSpec((1, d), lambda i, j: (0, 0)),
                  pl.BlockSpec((1, d), lambda i, j: (0, 0))],
        out_specs=[pl.BlockSpec((tm, d), lambda i, j: (i, 0)), pl.BlockSpec((tm, d), lambda i, j: (i, 0))],
        out_shape=[jax.ShapeDtypeStruct((m, d), F32), jax.ShapeDtypeStruct((m, d), MXU_DTYPE)],
        scratch_shapes=[pltpu.VMEM((tm, d), F32)],
        compiler_params=_params("arbitrary", "arbitrary"),
        name="ffn",
    )(x, xb, wu, wd, g2, b2)


def _expand_matrix(n_groups, width):
    c = np.arange(LANES).reshape(LANES, 1)
    j = np.arange(n_groups * width).reshape(1, -1) // width
    return jnp.asarray(c == j, dtype=MXU_DTYPE)


def _block_diag_q(q, n_heads, width, scale):
    b = q.shape[0]
    head = jnp.arange(n_heads * width) // width
    onehot = (head[:, None] == jnp.arange(LANES)[None, :]).astype(F32)
    return ((q.astype(F32) * scale)[:, :, None] * onehot[None]).astype(MXU_DTYPE)


def _select_kernel(qsel_ref, kcv_ref, e2_ref, oc_ref, idx_ref, *, pos_q, n_rows):
    half = KV_WIDTH
    kcv = kcv_ref[0]
    kc = kcv[:, :half].astype(MXU_DTYPE)
    vc = kcv[:, half:]
    n_idx = lax.broadcasted_iota(jnp.int32, (n_rows, LANES), 0)
    vis = ((n_idx + 1) * NSA_BLOCK - 1) <= pos_q
    imp = jnp.zeros((n_rows, LANES), F32)
    for g in range(NSA_GROUP):
        s = _dot(kc, qsel_ref[0, g])
        xm = jnp.where(vis, s, NEG_INF)
        mx = jnp.max(xm, axis=0, keepdims=True)
        e = jnp.where(vis, jnp.exp(xm - mx), 0.0)
        den = jnp.sum(e, axis=0, keepdims=True)
        p = e / jnp.where(den > 0.0, den, 1.0)
        imp = imp + p
        o = jnp.sum(_dot2(p, e2_ref[...]) * vc, axis=0, keepdims=True)
        for h in range(NSA_KV_HEADS):
            c = (h * NSA_GROUP + g) * HEAD_DIM
            oc_ref[0, :, c:c + HEAD_DIM] = o[:, h * HEAD_DIM:(h + 1) * HEAD_DIM]
    cur = pos_q // NSA_BLOCK
    n_f = n_idx.astype(F32)
    work = jnp.where(n_idx < cur, imp, -1.0)
    for k in range(NSA_TOPN - 1):
        mx = jnp.max(work, axis=0, keepdims=True)
        ix = jnp.min(jnp.where(work == mx, n_f, float(n_rows)), axis=0, keepdims=True)
        idx_ref[0, k:k + 1, :] = ix.astype(jnp.int32)
        work = jnp.where(n_f == ix, -2.0, work)
    idx_ref[0, NSA_TOPN - 1:NSA_TOPN, :] = jnp.full((1, LANES), cur, jnp.int32)


def _select(qsel, kcv, pos_q):
    nb, n_rows, width = kcv.shape
    e2 = _expand_matrix(NSA_KV_HEADS, HEAD_DIM)
    return pl.pallas_call(
        functools.partial(_select_kernel, pos_q=pos_q, n_rows=n_rows),
        grid=(nb,),
        in_specs=[pl.BlockSpec((1,) + qsel.shape[1:], lambda b: (b, 0, 0, 0)),
                  pl.BlockSpec((1, n_rows, width), lambda b: (b, 0, 0)),
                  pl.BlockSpec(e2.shape, lambda b: (0, 0))],
        out_specs=[pl.BlockSpec((1, 1, NSA_HEADS * HEAD_DIM), lambda b: (b, 0, 0)),
                   pl.BlockSpec((1, NSA_TOPN, LANES), lambda b: (b, 0, 0))],
        out_shape=[jax.ShapeDtypeStruct((nb, 1, NSA_HEADS * HEAD_DIM), F32),
                   jax.ShapeDtypeStruct((nb, NSA_TOPN, LANES), jnp.int32)],
        compiler_params=_params("arbitrary"),
        name="nsa_select",
    )(qsel, kcv, e2)


def _softmax_rows_plus_one(s, vis, s_new):
    xm = s if vis is None else jnp.where(vis, s, NEG_INF)
    mx = jnp.maximum(jnp.max(xm, axis=0, keepdims=True), s_new)
    e = jnp.exp(xm - mx)
    if vis is not None:
        e = jnp.where(vis, e, 0.0)
    e_new = jnp.exp(s_new - mx)
    den = jnp.sum(e, axis=0, keepdims=True) + e_new
    return e / den, e_new / den


def _nsa_sample_kernel(idx_ref, pt_ref, q_ref, oc_ref, slc_new_ref, win_new_ref, ws_ref, ng_ref, e4_ref,
                       cache_ref, out_ref, kvbuf, sem, *, layer, npg, wb):
    b = pl.program_id(0)
    n_old = NSA_TOPN - 1
    blk_rows = NSA_BLOCK * SLABS
    blocks_per_page = cache_ref.shape[2] // blk_rows

    def copies(fn):
        for h in range(NSA_KV_HEADS):
            for k in range(n_old):
                n = idx_ref[(b * NSA_TOPN + k) * NSA_KV_HEADS + h]
                page = pt_ref[b * npg + n // blocks_per_page]
                r0 = pl.multiple_of((n % blocks_per_page) * blk_rows, blk_rows)
                src = cache_ref.at[layer, page, pl.ds(r0, blk_rows), :]
                fn(pltpu.make_async_copy(src, kvbuf.at[h, pl.ds(k * blk_rows, blk_rows), :], sem.at[0]))

    copies(lambda cp: cp.start())
    copies(lambda cp: cp.wait())

    gw = NSA_GROUP * HEAD_DIM
    e4 = e4_ref[...]
    slc_new = slc_new_ref[0]
    win_new = win_new_ref[0]
    gates = ng_ref[0]
    n_keys = n_old * NSA_BLOCK
    i_w = lax.broadcasted_iota(jnp.int32, (wb, LANES), 0)
    vis_w = (wb - i_w) < WINDOW

    def attend(keys, vals, vis, k_new, v_new, qg):
        s = _dot(keys.astype(MXU_DTYPE), qg)
        s_new = _dot(jnp.concatenate([k_new] * 8, axis=0).astype(MXU_DTYPE), qg)[0:1]
        p, p_new = _softmax_rows_plus_one(s, vis, s_new)
        vt = jnp.concatenate([vals] * NSA_GROUP, axis=1)
        o = jnp.sum(_dot2(p, e4) * vt, axis=0, keepdims=True)
        o_new = _dot2(jnp.concatenate([p_new] * 8, axis=0), e4)[0:1] * jnp.concatenate([v_new] * NSA_GROUP, axis=1)
        return o + o_new

    for h in range(NSA_KV_HEADS):
        qg = q_ref[0, h]
        ks, vs = h * HEAD_DIM, KV_WIDTH + h * HEAD_DIM
        o_s = attend(_slab_rows(kvbuf.at[h], 0, n_keys, SLABS, [h]),
                     _slab_rows(kvbuf.at[h], 0, n_keys, SLABS, [NSA_KV_HEADS + h]), None,
                     slc_new[:, ks:ks + HEAD_DIM], slc_new[:, vs:vs + HEAD_DIM], qg)
        o_w = attend(_slab_rows(ws_ref.at[0], 0, wb, SLABS, [h]),
                     _slab_rows(ws_ref.at[0], 0, wb, SLABS, [NSA_KV_HEADS + h]), vis_w,
                     win_new[:, ks:ks + HEAD_DIM], win_new[:, vs:vs + HEAD_DIM], qg)
        o_c = oc_ref[0, :, h * gw:(h + 1) * gw]
        for g in range(NSA_GROUP):
            c = h * LANES + 3 * g
            sl = slice(g * HEAD_DIM, (g + 1) * HEAD_DIM)
            out_ref[0, :, h * gw + g * HEAD_DIM:h * gw + (g + 1) * HEAD_DIM] = (
                gates[:, c:c + 1] * o_c[:, sl] + gates[:, c + 1:c + 2] * o_s[:, sl]
                + gates[:, c + 2:c + 3] * o_w[:, sl])


def _nsa_sample(idx, table, qslc, o_c, slc_new, win_new, win_state, ng, cache, layer):
    nb, npg = table.shape
    wb = win_state.shape[2] // SLABS
    width = 2 * KV_WIDTH
    e4 = _expand_matrix(NSA_GROUP, HEAD_DIM)
    ws = win_state.reshape(win_state.shape[0] * nb, wb * SLABS, HEAD_DIM)
    idx_flat = idx[:, :, :NSA_KV_HEADS].reshape(-1)

    def one(width_):
        return pl.BlockSpec((1, 1, width_), lambda b, ix, pt: (b, 0, 0))

    return pl.pallas_call(
        functools.partial(_nsa_sample_kernel, layer=layer, npg=npg, wb=wb),
        grid_spec=pltpu.PrefetchScalarGridSpec(
            num_scalar_prefetch=2,
            grid=(nb,),
            in_specs=[pl.BlockSpec((1,) + qslc.shape[1:], lambda b, ix, pt: (b, 0, 0, 0)),
                      one(NSA_HEADS * HEAD_DIM), one(width), one(width),
                      pl.BlockSpec((1, wb * SLABS, HEAD_DIM), lambda b, ix, pt: (layer * nb + b, 0, 0)),
                      one(COL_TILE),
                      pl.BlockSpec(e4.shape, lambda b, ix, pt: (0, 0)),
                      pl.BlockSpec(memory_space=pl.ANY)],
            out_specs=one(NSA_HEADS * HEAD_DIM),
            scratch_shapes=[pltpu.VMEM((NSA_KV_HEADS, (NSA_TOPN - 1) * NSA_BLOCK * SLABS, HEAD_DIM), F32),
                            pltpu.SemaphoreType.DMA((1,))]),
        out_shape=jax.ShapeDtypeStruct((nb, 1, NSA_HEADS * HEAD_DIM), F32),
        compiler_params=_params("arbitrary"),
        name="nsa_sample",
    )(idx_flat, table.reshape(-1), qslc, o_c, slc_new, win_new, ws, ng, e4, cache)


PAGES_PER_STEP = 16
SB_DEC_KB = 512


def _page_pipeline(pt_ref, cache_ref, buf, sem, layer, npg, n_chunks, reverse):
    b = pl.program_id(0)
    c = pl.program_id(1)
    step = b * n_chunks + c
    slot = step % 2
    rpp = cache_ref.shape[2]

    def copies(bb, cc, sl, fn):
        first = (n_chunks - 1 - cc if reverse else cc) * PAGES_PER_STEP
        for i in range(PAGES_PER_STEP):
            page = pt_ref[bb * npg + first + i]
            fn(pltpu.make_async_copy(cache_ref.at[layer, page], buf.at[sl, pl.ds(i * rpp, rpp), :], sem.at[sl]))

    @pl.when(step == 0)
    def _():
        copies(b, c, slot, lambda cp: cp.start())

    @pl.when(step + 1 < pl.num_programs(0) * n_chunks)
    def _():
        nxt = step + 1
        copies(nxt // n_chunks, nxt % n_chunks, 1 - slot, lambda cp: cp.start())

    copies(b, c, slot, lambda cp: cp.wait())
    return slot


def _sb_sample_kernel(pt_ref, q_ref, lw_ref, e_ref, cache_ref, out_ref, buf, sem, tail_ref, acc_ref,
                      *, layer, npg, n_chunks):
    c = pl.program_id(1)
    slot = _page_pipeline(pt_ref, cache_ref, buf, sem, layer, npg, n_chunks, reverse=True)
    kw = SB_HEADS * HEAD_DIM
    kb = SB_DEC_KB

    @pl.when(c == 0)
    def _():
        tail_ref[...] = jnp.zeros_like(tail_ref)
        acc_ref[...] = jnp.zeros_like(acc_ref)

    qbd = q_ref[0]
    slabs = 2 * SB_HEADS
    n_keys = buf.shape[1] // slabs
    for j in reversed(range(n_keys // kb)):
        keys = _slab_rows(buf.at[slot], j * kb * slabs, kb, slabs, range(SB_HEADS))
        vals = _slab_rows(buf.at[slot], j * kb * slabs, kb, slabs, range(SB_HEADS, slabs))
        z = _dot(keys.astype(MXU_DTYPE), qbd)
        sp = _softplus(z)
        hi, lo = _split(-sp)
        suffix = _dot(lw_ref[...], hi) + _dot(lw_ref[...], lo)
        tail = tail_ref[0:1, :]
        a = jnp.exp(z - sp + suffix[:kb] + tail)
        contrib = _dot2(a, e_ref[...]) * vals
        acc_ref[...] += jnp.sum(contrib.reshape(kb // 8, 8, kw), axis=0)
        tail_ref[...] = tail_ref[...] + suffix[kb:kb + 8]

    @pl.when(c == n_chunks - 1)
    def _():
        out_ref[0] = jnp.sum(acc_ref[...], axis=0, keepdims=True)


def _suffix_matrix_rows(kb):
    s = np.arange(kb + 8).reshape(kb + 8, 1)
    j = np.arange(kb).reshape(1, kb)
    return jnp.asarray(np.logical_or(j > s, s >= kb), dtype=MXU_DTYPE)


def _sb_sample(table, qbd, cache, layer):
    nb, npg = table.shape
    flat_rows = cache.shape[2]
    rpp = flat_rows // (2 * SB_HEADS)
    assert npg % PAGES_PER_STEP == 0 and (PAGES_PER_STEP * rpp) % SB_DEC_KB == 0
    n_chunks = npg // PAGES_PER_STEP
    lw = _suffix_matrix_rows(SB_DEC_KB)
    e = _expand_matrix(SB_HEADS, HEAD_DIM)
    kw = SB_HEADS * HEAD_DIM
    return pl.pallas_call(
        functools.partial(_sb_sample_kernel, layer=layer, npg=npg, n_chunks=n_chunks),
        grid_spec=pltpu.PrefetchScalarGridSpec(
            num_scalar_prefetch=1,
            grid=(nb, n_chunks),
            in_specs=[pl.BlockSpec((1,) + qbd.shape[1:], lambda b, c, pt: (b, 0, 0)),
                      pl.BlockSpec(lw.shape, lambda b, c, pt: (0, 0)),
                      pl.BlockSpec(e.shape, lambda b, c, pt: (0, 0)),
                      pl.BlockSpec(memory_space=pl.ANY)],
            out_specs=pl.BlockSpec((1, 1, kw), lambda b, c, pt: (b, 0, 0)),
            scratch_shapes=[pltpu.VMEM((2, PAGES_PER_STEP * flat_rows, HEAD_DIM), F32),
                            pltpu.SemaphoreType.DMA((2,)),
                            pltpu.VMEM((8, LANES), F32),
                            pltpu.VMEM((8, kw), F32)]),
        out_shape=jax.ShapeDtypeStruct((nb, 1, kw), F32),
        compiler_params=_params("arbitrary", "arbitrary"),
        name="sb_sample",
    )(table.reshape(-1), qbd, lw, e, cache)


DIFF_DEC_KB = 512


def _diff_sample_kernel(pt_ref, q_ref, e0_ref, e1_ref, new_ref, lam_ref, g_ref, cache_ref, out_ref,
                        buf, sem, m_ref, l_ref, acc0_ref, acc1_ref, *, layer, npg, n_chunks, lam_init):
    c = pl.program_id(1)
    slot = _page_pipeline(pt_ref, cache_ref, buf, sem, layer, npg, n_chunks, reverse=False)
    kw = DIFF_HEADS * HEAD_DIM
    kb = DIFF_DEC_KB

    @pl.when(c == 0)
    def _():
        m_ref[...] = jnp.full_like(m_ref, NEG_INF)
        l_ref[...] = jnp.zeros_like(l_ref)
        acc0_ref[...] = jnp.zeros_like(acc0_ref)
        acc1_ref[...] = jnp.zeros_like(acc1_ref)

    qbd = q_ref[0]

    def absorb(keys, v, vis):
        n = keys.shape[0]
        s = _dot(keys.astype(MXU_DTYPE), qbd)
        if vis is not None:
            s = jnp.where(vis, s, NEG_INF)
        m_old = m_ref[...]
        m_new = jnp.maximum(m_old, jnp.max(s, axis=0, keepdims=True))
        alpha = jnp.exp(m_old - m_new)
        p = jnp.exp(s - m_new)
        if vis is not None:
            p = jnp.where(vis, p, 0.0)
        l_ref[...] = alpha * l_ref[...] + jnp.sum(p, axis=0, keepdims=True)
        m_ref[...] = m_new
        hi, lo = _split(p)
        a8 = jnp.concatenate([alpha] * 8, axis=0)
        for e_ref, acc_ref in ((e0_ref, acc0_ref), (e1_ref, acc1_ref)):
            pe = _dot(hi, e_ref[...]) + _dot(lo, e_ref[...])
            part = jnp.sum((pe * v).reshape(n // 8, 8, kw), axis=0)
            acc_ref[...] = acc_ref[...] * _dot2(a8, e_ref[...]) + part

    slabs = 2 * DIFF_HEADS
    n_keys = buf.shape[1] // slabs
    for j in range(n_keys // kb):
        absorb(_slab_rows(buf.at[slot], j * kb * slabs, kb, slabs, range(DIFF_HEADS)),
               _slab_rows(buf.at[slot], j * kb * slabs, kb, slabs, range(DIFF_HEADS, slabs)), None)

    @pl.when(c == n_chunks - 1)
    def _():
        new8 = jnp.concatenate([new_ref[0]] * 8, axis=0)
        first = lax.broadcasted_iota(jnp.int32, (8, LANES), 0) == 0
        absorb(new8[:, :kw], new8[:, kw:], first)
        l8 = jnp.concatenate([l_ref[...]] * 8, axis=0)
        o0 = jnp.sum(acc0_ref[...], axis=0, keepdims=True) / _dot2(l8, e0_ref[...])[0:1]
        o1 = jnp.sum(acc1_ref[...], axis=0, keepdims=True) / _dot2(l8, e1_ref[...])[0:1]
        lam = _diff_lambda(lam_ref, lam_init)
        for h in range(DIFF_HEADS):
            sl = slice(h * HEAD_DIM, (h + 1) * HEAD_DIM)
            out_ref[0, :, sl] = _diff_finish(o0[:, sl], o1[:, sl], lam, g_ref[...], lam_init)


def _diff_sample(table, qbd, new_rows, lam_rows, norm_g, cache, layer, lam_init):
    nb, npg = table.shape
    flat_rows = cache.shape[2]
    rpp = flat_rows // (2 * DIFF_HEADS)
    width = new_rows.shape[2]
    assert npg % PAGES_PER_STEP == 0 and (PAGES_PER_STEP * rpp) % DIFF_DEC_KB == 0
    n_chunks = npg // PAGES_PER_STEP
    kw = DIFF_HEADS * HEAD_DIM
    both = np.arange(LANES).reshape(LANES, 1)
    col_head = np.arange(kw).reshape(1, kw) // HEAD_DIM
    e0 = jnp.asarray(both == 2 * col_head, dtype=MXU_DTYPE)
    e1 = jnp.asarray(both == 2 * col_head + 1, dtype=MXU_DTYPE)
    return pl.pallas_call(
        functools.partial(_diff_sample_kernel, layer=layer, npg=npg, n_chunks=n_chunks, lam_init=lam_init),
        grid_spec=pltpu.PrefetchScalarGridSpec(
            num_scalar_prefetch=1,
            grid=(nb, n_chunks),
            in_specs=[pl.BlockSpec((1,) + qbd.shape[1:], lambda b, c, pt: (b, 0, 0)),
                      pl.BlockSpec(e0.shape, lambda b, c, pt: (0, 0)),
                      pl.BlockSpec(e1.shape, lambda b, c, pt: (0, 0)),
                      pl.BlockSpec((1, 1, width), lambda b, c, pt: (b, 0, 0)),
                      pl.BlockSpec(lam_rows.shape, lambda b, c, pt: (0, 0)),
                      pl.BlockSpec((1, HEAD_DIM), lambda b, c, pt: (0, 0)),
                      pl.BlockSpec(memory_space=pl.ANY)],
            out_specs=pl.BlockSpec((1, 1, kw), lambda b, c, pt: (b, 0, 0)),
            scratch_shapes=[pltpu.VMEM((2, PAGES_PER_STEP * flat_rows, HEAD_DIM), F32),
                            pltpu.SemaphoreType.DMA((2,)),
                            pltpu.VMEM((1, LANES), F32), pltpu.VMEM((1, LANES), F32),
                            pltpu.VMEM((8, kw), F32), pltpu.VMEM((8, kw), F32)]),
        out_shape=jax.ShapeDtypeStruct((nb, 1, kw), F32),
        compiler_params=_params("arbitrary", "arbitrary"),
        name="diff_sample",
    )(table.reshape(-1), qbd, e0, e1, new_rows, lam_rows, norm_g.reshape(1, HEAD_DIM), cache)


SAMPLE_ROWS = 16


def _pick_tm(m, cap):
    tm = min(m, cap)
    while m % tm:
        tm //= 2
    return tm


def kernel(x_prompt, x_sample, cache_nsa_cmp_kv, cache_nsa_slc_kv, state_nsa_win_kv, cache_sb_kv,
           cache_diff_kv, page_table, w_in, w_o, ln1_g, ln1_b, ln2_g, ln2_b, w_up, w_down,
           nsa_cmp_pos, nsa_cmp_w1, nsa_cmp_w2, diff_lambda, diff_norm_g):
    depth = w_in.shape[0]
    batch, seq, d_model = x_prompt.shape
    nbs, dec_seq, _ = x_sample.shape
    assert dec_seq == 1 and nbs <= SAMPLE_ROWS
    npg = page_table.shape[1]
    n_pool, page = cache_nsa_cmp_kv.shape[1], cache_nsa_cmp_kv.shape[2]
    past = npg * page
    alpha = (2 * depth) ** 0.25
    width = 2 * KV_WIDTH
    mp = batch * seq
    tm_p = _pick_tm(seq, 1024)

    tabs_p = (_rope_tables(jnp.arange(seq), HEAD_DIM, HEAD_DIM // ROT_FRACTION)
              + _rope_tables(jnp.arange(seq), DIFF_DIM, DIFF_DIM // ROT_FRACTION))
    pos_s = jnp.full((SAMPLE_ROWS,), past, jnp.int32)
    tabs_s = (_rope_tables(pos_s, HEAD_DIM, HEAD_DIM // ROT_FRACTION)
              + _rope_tables(pos_s, DIFF_DIM, DIFF_DIM // ROT_FRACTION))

    cmp_cache = cache_nsa_cmp_kv.reshape(depth, n_pool, page * SLABS, HEAD_DIM)
    slc_cache = cache_nsa_slc_kv.reshape(depth, n_pool, page * SLABS, HEAD_DIM)
    sb_cache = cache_sb_kv.reshape(depth, n_pool, page * 2 * SB_HEADS, HEAD_DIM)
    diff_cache = cache_diff_kv.reshape(depth, n_pool, page * 2 * DIFF_HEADS, HEAD_DIM)
    win_state = state_nsa_win_kv.reshape(depth, nbs, state_nsa_win_kv.shape[2] * SLABS, HEAD_DIM)
    prompt_pages = jnp.arange(mp // page, dtype=jnp.int32).reshape(1, mp // page)
    new_pages = jnp.arange(nbs, dtype=jnp.int32).reshape(1, nbs)

    xp = x_prompt.reshape(mp, d_model)
    xs = jnp.pad(x_sample.reshape(nbs, d_model), ((0, SAMPLE_ROWS - nbs), (0, 0)))
    xp_b, xs_b = xp.astype(MXU_DTYPE), xs.astype(MXU_DTYPE)
    st_p = [[] for _ in range(5)]
    st_s = [[] for _ in range(5)]

    for l in range(depth):
        lam_init = 0.8 - 0.6 * math.exp(-0.3 * l)
        w_p = _permute_w_in(w_in[l])
        wo_b, wu_b, wd_b = w_o[l].astype(MXU_DTYPE), w_up[l].astype(MXU_DTYPE), w_down[l].astype(MXU_DTYPE)
        posc, w1bd, w2bd = _compress_weights(nsa_cmp_pos[l], nsa_cmp_w1[l], nsa_cmp_w2[l])
        lam_rows = diff_lambda[l].astype(F32)

        (qraw, qrot, cmp, slc, win, sbq, sbkv, dq, dkv, mg, ng) = _project(xp_b, w_p, tabs_p, tm_p, seq)
        kcv = _compress(cmp.reshape(1, mp // page, page * SLABS, HEAD_DIM), 0, prompt_pages, posc, w1bd, w2bd)
        o_nsa = _nsa_prompt(qraw, qrot, kcv, slc, win, ng, batch, seq)
        o_sb = _sb_prompt(sbq, sbkv, batch, seq)
        o_diff = _diff_prompt(dq, dkv, lam_rows, diff_norm_g[l], lam_init, batch, seq)
        x1, x1b = _merge(o_nsa, o_sb, o_diff, mg, xp, wo_b, ln1_g[l], ln1_b[l], alpha, _pick_tm(mp, 256))
        xp, xp_b = _ffn(x1, x1b, wu_b, wd_b, ln2_g[l], ln2_b[l], alpha, _pick_tm(mp, 512))
        win_rows = min(WINDOW, seq)
        st_p[0].append(cmp.reshape(batch, seq, 2, NSA_KV_HEADS, HEAD_DIM))
        st_p[1].append(slc.reshape(batch, seq, 2, NSA_KV_HEADS, HEAD_DIM))
        st_p[2].append(win.reshape(batch, seq, 2, NSA_KV_HEADS, HEAD_DIM)[:, seq - win_rows:])
        st_p[3].append(sbkv.reshape(batch, seq, 2, SB_HEADS, HEAD_DIM))
        st_p[4].append(dkv.reshape(batch, seq, 2, DIFF_HEADS, HEAD_DIM))

        (qraw, qrot, cmp, slc, win, sbq, sbkv, dq, dkv, mg, ng) = _project(
            xs_b, w_p, tabs_s, SAMPLE_ROWS, SAMPLE_ROWS)
        kcv_past = _compress(cmp_cache, l, page_table, posc, w1bd, w2bd)
        new_blk = jnp.zeros((1, nbs, NSA_BLOCK, width), F32).at[0, :, 0, :].set(cmp[:nbs])
        kcv_new = _compress(new_blk.reshape(1, nbs, NSA_BLOCK * SLABS, HEAD_DIM), 0, new_pages, posc, w1bd, w2bd)
        n_past = kcv_past.shape[1]
        n_rows = -(-(n_past + 1) // 8) * 8
        kcv_all = jnp.concatenate([kcv_past, kcv_new.reshape(nbs, 1, width),
                                   jnp.zeros((nbs, n_rows - n_past - 1, width), F32)], axis=1)
        scale = HEAD_DIM ** -0.5
        q4 = qraw[:nbs].reshape(nbs, NSA_KV_HEADS, NSA_GROUP, HEAD_DIM).transpose(0, 2, 1, 3)
        qsel = jnp.stack([_block_diag_q(q4[:, g].reshape(nbs, KV_WIDTH), NSA_KV_HEADS, HEAD_DIM, scale)
                          for g in range(NSA_GROUP)], axis=1)
        o_c, idx = _select(qsel, kcv_all, past)
        qr = qrot[:nbs].reshape(nbs, NSA_KV_HEADS, NSA_GROUP, HEAD_DIM).astype(F32) * scale
        qslc = jnp.pad(qr.transpose(0, 1, 3, 2), ((0, 0), (0, 0), (0, 0), (0, LANES - NSA_GROUP))).astype(MXU_DTYPE)
        o_nsa = _nsa_sample(idx, page_table, qslc, o_c, slc[:nbs].reshape(nbs, 1, width),
                            win[:nbs].reshape(nbs, 1, width), win_state, ng[:nbs].reshape(nbs, 1, COL_TILE),
                            slc_cache, l)
        o_sb = _sb_sample(page_table, _block_diag_q(sbq[:nbs], SB_HEADS, HEAD_DIM, scale), sb_cache, l)
        o_diff = _diff_sample(page_table, _block_diag_q(dq[:nbs], 2 * DIFF_HEADS, DIFF_DIM, DIFF_DIM ** -0.5),
                              dkv[:nbs].reshape(nbs, 1, -1), lam_rows, diff_norm_g[l], diff_cache, l, lam_init)

        def pad_rows(a):
            return jnp.pad(a.reshape(nbs, -1), ((0, SAMPLE_ROWS - nbs), (0, 0)))

        x1, x1b = _merge(pad_rows(o_nsa), pad_rows(o_sb), pad_rows(o_diff), mg, xs, wo_b, ln1_g[l], ln1_b[l],
                         alpha, SAMPLE_ROWS)
        xs, xs_b = _ffn(x1, x1b, wu_b, wd_b, ln2_g[l], ln2_b[l], alpha, SAMPLE_ROWS)
        new_win = win[:nbs].reshape(nbs, 1, 2, NSA_KV_HEADS, HEAD_DIM)
        wb = state_nsa_win_kv.shape[2]
        keep = min(WINDOW, past + 1)
        st_s[0].append(cmp[:nbs].reshape(nbs, 1, 2, NSA_KV_HEADS, HEAD_DIM))
        st_s[1].append(slc[:nbs].reshape(nbs, 1, 2, NSA_KV_HEADS, HEAD_DIM))
        st_s[2].append(jnp.concatenate([state_nsa_win_kv[l], new_win], axis=1)[:, wb + 1 - keep:])
        st_s[3].append(sbkv[:nbs].reshape(nbs, 1, 2, SB_HEADS, HEAD_DIM))
        st_s[4].append(dkv[:nbs].reshape(nbs, 1, 2, DIFF_HEADS, HEAD_DIM))

    return (xp.reshape(batch, seq, d_model), xs[:nbs].reshape(nbs, 1, d_model),
            jnp.stack(st_p[0]), jnp.stack(st_s[0]),
            jnp.stack(st_p[1]), jnp.stack(st_s[1]),
            jnp.stack(st_p[2]), jnp.stack(st_s[2]),
            jnp.stack(st_p[3]), jnp.stack(st_s[3]),
            jnp.stack(st_p[4]), jnp.stack(st_s[4]))
```

```python
import functools
import math

import numpy as np
import jax
import jax.numpy as jnp
from jax import lax
from jax.experimental import pallas as pl
from jax.experimental.pallas import tpu as pltpu

HEAD_DIM = 128
NSA_HEADS = 8
NSA_KV_HEADS = 2
NSA_GROUP = NSA_HEADS // NSA_KV_HEADS
NSA_BLOCK = 64
NSA_TOPN = 16
WINDOW = 512
SB_HEADS = 4
DIFF_HEADS = 4
DIFF_DIM = HEAD_DIM // 2
ROPE_THETA = 500000.0
ROT_FRACTION = 4
LN_EPS = 1e-5
NEG_INF = -1e30

F32 = jnp.float32
MXU_DTYPE = jnp.bfloat16

LANES = 128
COL_TILE = 256
VMEM_LIMIT_BYTES = 56 * 2**20

KV_WIDTH = NSA_KV_HEADS * HEAD_DIM
Q_TILES = NSA_HEADS * HEAD_DIM // COL_TILE
T_CMP = Q_TILES
T_SLC = T_CMP + 2
T_WIN = T_SLC + 2
T_SBQ = T_WIN + 2
T_SBKV = T_SBQ + SB_HEADS * HEAD_DIM // COL_TILE
T_DQ = T_SBKV + 2 * SB_HEADS * HEAD_DIM // COL_TILE
T_DKV = T_DQ + DIFF_HEADS * HEAD_DIM // COL_TILE
T_MG = T_DKV + 2 * DIFF_HEADS * HEAD_DIM // COL_TILE
MIX_WIDTH = (NSA_HEADS + SB_HEADS + DIFF_HEADS) * HEAD_DIM
T_NG = T_MG + MIX_WIDTH // COL_TILE
N_TILES = T_NG + 1


def _dot(a, b):
    return jnp.dot(a, b, preferred_element_type=F32)


def _dot_nt(a, b):
    return lax.dot_general(a, b, (((1,), (1,)), ((), ())), preferred_element_type=F32)


def _split(x):
    hi = x.astype(MXU_DTYPE)
    lo = (x - hi.astype(F32)).astype(MXU_DTYPE)
    return hi, lo


def _dot2(x, w):
    hi, lo = _split(x)
    return _dot(hi, w) + _dot(lo, w)


def _params(*sem):
    return pltpu.CompilerParams(dimension_semantics=sem, vmem_limit_bytes=VMEM_LIMIT_BYTES)


def _softplus(z):
    return jnp.maximum(z, 0.0) + jnp.log(1.0 + jnp.exp(-jnp.abs(z)))


def _layer_norm(h, g, b):
    mu = jnp.mean(h, axis=-1, keepdims=True)
    d = h - mu
    var = jnp.mean(d * d, axis=-1, keepdims=True)
    return d * lax.rsqrt(var + LN_EPS) * g + b


def _rope_tables(pos, width, rot_dim):
    half = rot_dim // 2
    inv_freq = ROPE_THETA ** (-jnp.arange(half, dtype=F32) / half)
    ang = pos.astype(F32)[:, None] * inv_freq[None, :]
    cos, sin = jnp.cos(ang), jnp.sin(ang)
    n = pos.shape[0]
    rest = width - 2 * half
    c = jnp.concatenate([cos, cos, jnp.ones((n, rest), F32)], axis=1)
    s_up = jnp.concatenate([jnp.zeros((n, half), F32), sin, jnp.zeros((n, rest), F32)], axis=1)
    s_dn = jnp.concatenate([-sin, jnp.zeros((n, half + rest), F32)], axis=1)
    rep = LANES // width
    return tuple(jnp.tile(a, (1, rep)) for a in (c, s_up, s_dn))


def _proj_kernel(x_ref, w_ref, c_ref, su_ref, sd_ref, cd_ref, sud_ref, sdd_ref,
                 qraw_ref, qrot_ref, cmp_ref, slc_ref, win_ref, sbq_ref, sbkv_ref,
                 dq_ref, dkv_ref, mg_ref, ng_ref):
    n = pl.program_id(1)
    acc = _dot(x_ref[...], w_ref[...])

    def rope(a, c, su, sd, shift):
        def one(h):
            return (h * c[...] + pltpu.roll(h, shift, 1) * su[...]
                    + pltpu.roll(h, LANES - shift, 1) * sd[...])
        return jnp.concatenate([one(a[:, :LANES]), one(a[:, LANES:])], axis=1)

    def rope_head(a):
        return rope(a, c_ref, su_ref, sd_ref, HEAD_DIM // ROT_FRACTION // 2)

    def rope_diff(a):
        return rope(a, cd_ref, sud_ref, sdd_ref, DIFF_DIM // ROT_FRACTION // 2)

    def between(lo, hi):
        return jnp.logical_and(n >= lo, n < hi)

    @pl.when(n < T_CMP)
    def _():
        qraw_ref[...] = acc.astype(qraw_ref.dtype)
        qrot_ref[...] = rope_head(acc).astype(qrot_ref.dtype)

    @pl.when(between(T_CMP, T_SLC))
    def _():
        cmp_ref[...] = acc

    @pl.when(n == T_SLC)
    def _():
        slc_ref[...] = rope_head(acc)

    @pl.when(n == T_SLC + 1)
    def _():
        slc_ref[...] = acc

    @pl.when(n == T_WIN)
    def _():
        win_ref[...] = rope_head(acc)

    @pl.when(n == T_WIN + 1)
    def _():
        win_ref[...] = acc

    @pl.when(between(T_SBQ, T_SBKV))
    def _():
        sbq_ref[...] = acc.astype(sbq_ref.dtype)

    @pl.when(between(T_SBKV, T_DQ))
    def _():
        sbkv_ref[...] = acc

    @pl.when(between(T_DQ, T_DKV))
    def _():
        dq_ref[...] = rope_diff(acc).astype(dq_ref.dtype)

    @pl.when(between(T_DKV, T_DKV + 2))
    def _():
        dkv_ref[...] = rope_diff(acc)

    @pl.when(between(T_DKV + 2, T_MG))
    def _():
        dkv_ref[...] = acc

    @pl.when(between(T_MG, T_NG))
    def _():
        mg_ref[...] = jax.nn.sigmoid(acc)

    @pl.when(n == T_NG)
    def _():
        ng_ref[...] = jax.nn.sigmoid(acc)


def _project(xb, w_p, tabs, tm, rows_per_seq):
    m, d = xb.shape
    seq_tiles = rows_per_seq // tm

    def omap(start, cnt):
        return lambda i, n: (i, jnp.clip(n - start, 0, cnt - 1))

    def out(width, dtype, start, cnt):
        return (jax.ShapeDtypeStruct((m, width), dtype), pl.BlockSpec((tm, COL_TILE), omap(start, cnt)))

    outs = [
        out(NSA_HEADS * HEAD_DIM, MXU_DTYPE, 0, Q_TILES),
        out(NSA_HEADS * HEAD_DIM, MXU_DTYPE, 0, Q_TILES),
        out(2 * KV_WIDTH, F32, T_CMP, 2),
        out(2 * KV_WIDTH, F32, T_SLC, 2),
        out(2 * KV_WIDTH, F32, T_WIN, 2),
        out(SB_HEADS * HEAD_DIM, MXU_DTYPE, T_SBQ, T_SBKV - T_SBQ),
        out(2 * SB_HEADS * HEAD_DIM, F32, T_SBKV, T_DQ - T_SBKV),
        out(DIFF_HEADS * HEAD_DIM, MXU_DTYPE, T_DQ, T_DKV - T_DQ),
        out(2 * DIFF_HEADS * HEAD_DIM, F32, T_DKV, T_MG - T_DKV),
        out(MIX_WIDTH, F32, T_MG, T_NG - T_MG),
        out(COL_TILE, F32, T_NG, 1),
    ]
    tab_spec = pl.BlockSpec((tm, LANES), lambda i, n: (i % seq_tiles, 0))
    return pl.pallas_call(
        _proj_kernel,
        grid=(m // tm, N_TILES),
        in_specs=[pl.BlockSpec((tm, d), lambda i, n: (i, 0)),
                  pl.BlockSpec((d, COL_TILE), lambda i, n: (0, n))] + [tab_spec] * 6,
        out_specs=[o[1] for o in outs],
        out_shape=[o[0] for o in outs],
        compiler_params=_params("arbitrary", "arbitrary"),
        name="proj_in",
    )(xb, w_p, *tabs)


def _permute_w_in(w):
    d = w.shape[0]
    g0 = Q_TILES * COL_TILE + 6 * KV_WIDTH
    ng = 3 * NSA_HEADS
    per = ng // NSA_KV_HEADS
    gate = w[:, g0:g0 + ng]
    pad = jnp.zeros((d, LANES - per), w.dtype)
    gate_tile = jnp.concatenate([gate[:, :per], pad, gate[:, per:], pad], axis=1)
    return jnp.concatenate([w[:, :g0], w[:, g0 + ng:], gate_tile], axis=1).astype(MXU_DTYPE)


R_CHUNK = 16
SLABS = 2 * NSA_KV_HEADS


def _slab_rows(ref, first, count, stride, slabs):
    return jnp.concatenate([ref[pl.ds(first + c, count, stride=stride), :] for c in slabs], axis=1)


def _gelu(x):
    return 0.5 * x * (1.0 + jnp.tanh(math.sqrt(2.0 / math.pi) * (x + 0.044715 * (x * x * x))))


def _compress_kernel(pt_ref, pos_ref, w1_ref, w2_ref, cache_ref, out_ref, buf, sem, acc_ref,
                     *, layer, npg, bpp, n_steps):
    b = pl.program_id(0)
    j = pl.program_id(1)
    n_chunks = NSA_BLOCK // R_CHUNK
    step = b * n_chunks + j
    slot = step % 2
    nblk = npg * bpp

    def copy(bb, jj, sl, i, h):
        page = pt_ref[bb * npg + i]
        src = cache_ref.at[layer, page, pl.ds((h * NSA_BLOCK + jj * R_CHUNK) * SLABS, R_CHUNK * SLABS), :]
        dst = buf.at[sl, pl.ds((i * bpp + h) * R_CHUNK * SLABS, R_CHUNK * SLABS), :]
        return pltpu.make_async_copy(src, dst, sem.at[sl])

    def for_all(bb, jj, sl, start):
        def body(i, carry):
            for h in range(bpp):
                cp = copy(bb, jj, sl, i, h)
                if start:
                    cp.start()
                else:
                    cp.wait()
            return carry
        lax.fori_loop(0, npg, body, 0)

    @pl.when(step == 0)
    def _():
        for_all(b, j, slot, True)

    @pl.when(step + 1 < n_steps)
    def _():
        nxt = step + 1
        for_all(nxt // n_chunks, nxt % n_chunks, 1 - slot, True)

    for_all(b, j, slot, False)

    @pl.when(j == 0)
    def _():
        acc_ref[...] = jnp.zeros_like(acc_ref)

    half = 2 * HEAD_DIM
    a_k = jnp.zeros((nblk, half), F32)
    a_v = jnp.zeros((nblk, half), F32)
    for r in range(R_CHUNK):
        xr = _slab_rows(buf.at[slot], r * SLABS, nblk, R_CHUNK * SLABS, range(SLABS))
        xr = xr + pos_ref[pl.ds(j * R_CHUNK + r, 1), :]
        xb = xr.astype(MXU_DTYPE)
        a_k = a_k + _dot(xb[:, :half], w1_ref[r, 0])
        a_v = a_v + _dot(xb[:, half:], w1_ref[r, 1])
    acc_ref[:, :half] += a_k
    acc_ref[:, half:] += a_v

    @pl.when(j == n_chunks - 1)
    def _():
        h = _gelu(acc_ref[...]).astype(MXU_DTYPE)
        out_ref[0] = _dot(h, w2_ref[...])


def _compress(cache, layer, table, posc, w1bd, w2bd):
    nb, npg = table.shape
    rpp = cache.shape[2] // SLABS
    bpp = rpp // NSA_BLOCK
    nblk = npg * bpp
    n_chunks = NSA_BLOCK // R_CHUNK
    width = 2 * KV_WIDTH
    kern = functools.partial(_compress_kernel, layer=layer, npg=npg, bpp=bpp, n_steps=nb * n_chunks)
    return pl.pallas_call(
        kern,
        grid_spec=pltpu.PrefetchScalarGridSpec(
            num_scalar_prefetch=1,
            grid=(nb, n_chunks),
            in_specs=[pl.BlockSpec((NSA_BLOCK, width), lambda b, j, pt: (0, 0)),
                      pl.BlockSpec((R_CHUNK,) + w1bd.shape[1:], lambda b, j, pt: (j, 0, 0, 0)),
                      pl.BlockSpec(w2bd.shape, lambda b, j, pt: (0, 0)),
                      pl.BlockSpec(memory_space=pl.ANY)],
            out_specs=pl.BlockSpec((1, nblk, width), lambda b, j, pt: (b, 0, 0)),
            scratch_shapes=[pltpu.VMEM((2, nblk * R_CHUNK * SLABS, HEAD_DIM), F32),
                            pltpu.SemaphoreType.DMA((2,)),
                            pltpu.VMEM((nblk, width), F32)]),
        out_shape=jax.ShapeDtypeStruct((nb, nblk, width), F32),
        compiler_params=_params("arbitrary", "arbitrary"),
        name="nsa_compress",
    )(table.reshape(-1), posc, w1bd, w2bd, cache)


def _compress_weights(pos, w1, w2):
    posc = jnp.concatenate([pos[0]] * NSA_KV_HEADS + [pos[1]] * NSA_KV_HEADS, axis=1)
    w1r = w1.reshape(2, NSA_BLOCK, HEAD_DIM, HEAD_DIM)
    z = jnp.zeros_like(w1r)
    w1bd = jnp.concatenate([jnp.concatenate([w1r, z], axis=3), jnp.concatenate([z, w1r], axis=3)], axis=2)
    w1bd = jnp.transpose(w1bd, (1, 0, 2, 3)).astype(MXU_DTYPE)
    blocks = [w2[0]] * NSA_KV_HEADS + [w2[1]] * NSA_KV_HEADS
    n = len(blocks)
    w2bd = jnp.concatenate(
        [jnp.concatenate([blk if c == r else jnp.zeros_like(blk) for c in range(n)], axis=1)
         for r, blk in enumerate(blocks)], axis=0).astype(MXU_DTYPE)
    return posc, w1bd, w2bd


NSA_TQ = 256
NSA_KB = 256


def _fill_kv_scratch(k_ref, v_ref, kb_ref, vt_ref, seq, chunk):
    for c in range(seq // chunk):
        rows = slice(c * chunk, (c + 1) * chunk)
        kb_ref[rows, :] = k_ref[rows, :].astype(MXU_DTYPE)
        vt_ref[:, rows] = v_ref[rows, :].T.astype(MXU_DTYPE)


def _online_softmax_step(xm, vt, m_ref, l_ref, acc_ref):
    m_old = m_ref[...]
    m_new = jnp.maximum(m_old, jnp.max(xm, axis=0, keepdims=True))
    alpha = jnp.exp(m_old - m_new)
    p = jnp.exp(xm - jnp.maximum(m_new, 0.1 * NEG_INF))
    l_ref[...] = alpha * l_ref[...] + jnp.sum(p, axis=0, keepdims=True)
    acc_ref[...] = alpha * acc_ref[...] + _dot(vt, p.astype(MXU_DTYPE))
    m_ref[...] = m_new


def _nsa_prompt_kernel(qraw_ref, qrot_ref, kc_ref, vc_ref, sk_ref, sv_ref, wk_ref, wv_ref, ng_ref,
                       eblk_ref, out_ref, skb_ref, svt_ref, wkb_ref, wvt_ref, m_ref, l_ref, acc_ref,
                       *, seq, nb):
    qi = pl.program_id(2)
    tq = NSA_TQ
    g4 = NSA_GROUP
    nq = g4 * tq
    t0 = qi * tq
    scale = HEAD_DIM ** -0.5
    n_sel = min(NSA_TOPN, nb)

    @pl.when(qi == 0)
    def _():
        _fill_kv_scratch(sk_ref, sv_ref, skb_ref, svt_ref, seq, NSA_KB)
        _fill_kv_scratch(wk_ref, wv_ref, wkb_ref, wvt_ref, seq, NSA_KB)

    def stack(ref):
        return jnp.concatenate([ref[:, g * HEAD_DIM:(g + 1) * HEAD_DIM] for g in range(g4)], axis=0)

    def tile_lanes(a):
        return jnp.concatenate([a] * g4, axis=1)

    qraw = stack(qraw_ref)
    qrot = stack(qrot_ref)
    t_q = t0 + lax.broadcasted_iota(jnp.int32, (1, tq), 1)
    t_lane = tile_lanes(t_q)

    kc = kc_ref[...].astype(MXU_DTYPE)
    s_c = (_dot_nt(kc, qraw) * scale)[0:nb]
    blk = lax.broadcasted_iota(jnp.int32, (nb, 1), 0)
    vis_c = ((blk + 1) * NSA_BLOCK - 1) <= t_lane
    x_c = jnp.where(vis_c, s_c, NEG_INF)
    e_c = jnp.where(vis_c, jnp.exp(x_c - jnp.max(x_c, axis=0, keepdims=True)), 0.0)
    den = jnp.sum(e_c, axis=0, keepdims=True)
    p_c = e_c / jnp.where(den > 0.0, den, 1.0)
    vc_t = vc_ref[...].T[:, 0:nb].astype(MXU_DTYPE)
    o_c = _dot(vc_t, p_c.astype(MXU_DTYPE))

    imp = p_c[:, 0:tq]
    for g in range(1, g4):
        imp = imp + p_c[:, g * tq:(g + 1) * tq]
    cur = t_q // NSA_BLOCK
    n_idx = lax.broadcasted_iota(jnp.int32, (nb, tq), 0)
    rank = jnp.zeros((nb, tq), jnp.int32)
    for mm in range(nb):
        row = imp[mm:mm + 1, :]
        ahead = jnp.where(row > imp, 1, jnp.where(row == imp, jnp.where(mm < n_idx, 1, 0), 0))
        rank = rank + jnp.where(mm < cur, ahead, 0)
    sel = jnp.where(n_idx == cur, 1.0, jnp.where(n_idx < cur, jnp.where(rank < n_sel - 1, 1.0, 0.0), 0.0))
    sel = jnp.concatenate([sel, jnp.zeros((LANES - nb, tq), F32)], axis=0).astype(MXU_DTYPE)

    m_ref[...] = jnp.full_like(m_ref, NEG_INF)
    l_ref[...] = jnp.zeros_like(l_ref)
    acc_ref[...] = jnp.zeros_like(acc_ref)
    n_chunks = (t0 + tq - 1) // NSA_KB + 1

    def slc_body(kj, carry):
        k0 = pl.multiple_of(kj * NSA_KB, NSA_KB)
        s = _dot_nt(skb_ref[pl.ds(k0, NSA_KB), :], qrot) * scale
        selm = tile_lanes(_dot(eblk_ref[kj], sel))
        kpos = k0 + lax.broadcasted_iota(jnp.int32, (NSA_KB, 1), 0)
        xm = jnp.where(selm > 0.5, jnp.where(kpos <= t_lane, s, NEG_INF), NEG_INF)
        _online_softmax_step(xm, svt_ref[:, pl.ds(k0, NSA_KB)], m_ref, l_ref, acc_ref)
        return carry

    lax.fori_loop(0, n_chunks, slc_body, 0)
    o_s = acc_ref[...] / l_ref[...]

    band = min(WINDOW + tq, seq)
    w0 = pl.multiple_of(jnp.clip(t0 + tq - band, 0, seq - band), tq)
    s_w = _dot_nt(wkb_ref[pl.ds(w0, band), :], qrot) * scale
    dpos = t_lane - (w0 + lax.broadcasted_iota(jnp.int32, (band, 1), 0))
    x_w = jnp.where(dpos >= 0, jnp.where(dpos < WINDOW, s_w, NEG_INF), NEG_INF)
    e_w = jnp.exp(x_w - jnp.max(x_w, axis=0, keepdims=True))
    o_w = _dot(wvt_ref[:, pl.ds(w0, band)], e_w.astype(MXU_DTYPE)) / jnp.sum(e_w, axis=0, keepdims=True)

    gates = ng_ref[...].T
    for g in range(g4):
        r = slice(g * tq, (g + 1) * tq)
        o = (gates[3 * g:3 * g + 1] * o_c[:, r] + gates[3 * g + 1:3 * g + 2] * o_s[:, r]
             + gates[3 * g + 2:3 * g + 3] * o_w[:, r])
        out_ref[:, g * HEAD_DIM:(g + 1) * HEAD_DIM] = o.T


def _nsa_prompt(qraw, qrot, kcv, slc, win, ng, batch, seq):
    m = qraw.shape[0]
    nb = seq // NSA_BLOCK
    assert nb <= LANES and seq % NSA_KB == 0 and seq % NSA_TQ == 0
    nq = seq // NSA_TQ
    nkc = seq // NSA_KB
    kcv_p = jnp.pad(kcv.reshape(batch, nb, 2 * KV_WIDTH), ((0, 0), (0, LANES - nb), (0, 0)))
    kpos = np.arange(seq).reshape(nkc, NSA_KB, 1) // NSA_BLOCK
    eblk = jnp.asarray(kpos == np.arange(LANES).reshape(1, 1, LANES), dtype=MXU_DTYPE)
    gw = NSA_GROUP * HEAD_DIM
    lanes = NSA_GROUP * NSA_TQ
    q_spec = pl.BlockSpec((NSA_TQ, gw), lambda b, h, q: (b * nq + q, h))

    def kv_spec(off):
        return pl.BlockSpec((seq, HEAD_DIM), lambda b, h, q: (b, off + h))

    return pl.pallas_call(
        functools.partial(_nsa_prompt_kernel, seq=seq, nb=nb),
        grid=(batch, NSA_KV_HEADS, nq),
        in_specs=[q_spec, q_spec,
                  pl.BlockSpec((None, LANES, HEAD_DIM), lambda b, h, q: (b, 0, h)),
                  pl.BlockSpec((None, LANES, HEAD_DIM), lambda b, h, q: (b, 0, NSA_KV_HEADS + h)),
                  kv_spec(0), kv_spec(NSA_KV_HEADS), kv_spec(0), kv_spec(NSA_KV_HEADS),
                  pl.BlockSpec((NSA_TQ, LANES), lambda b, h, q: (b * nq + q, h)),
                  pl.BlockSpec(eblk.shape, lambda b, h, q: (0, 0, 0))],
        out_specs=pl.BlockSpec((NSA_TQ, gw), lambda b, h, q: (b * nq + q, h)),
        out_shape=jax.ShapeDtypeStruct((m, NSA_HEADS * HEAD_DIM), F32),
        scratch_shapes=[pltpu.VMEM((seq, HEAD_DIM), MXU_DTYPE), pltpu.VMEM((HEAD_DIM, seq), MXU_DTYPE),
                        pltpu.VMEM((seq, HEAD_DIM), MXU_DTYPE), pltpu.VMEM((HEAD_DIM, seq), MXU_DTYPE),
                        pltpu.VMEM((1, lanes), F32), pltpu.VMEM((1, lanes), F32),
                        pltpu.VMEM((HEAD_DIM, lanes), F32)],
        compiler_params=_params("arbitrary", "arbitrary", "arbitrary"),
        name="nsa_prompt",
    )(qraw, qrot, kcv_p, kcv_p, slc, slc, win, win, ng, eblk)


SB_TQ = 512
SB_KB = 256


def _suffix_matrix_rows(kb):
    s = np.arange(kb + 8).reshape(kb + 8, 1)
    j = np.arange(kb).reshape(1, kb)
    return jnp.asarray(np.logical_or(j > s, s >= kb), dtype=MXU_DTYPE)


def _sb_prompt_kernel(q_ref, k_ref, v_ref, lw_ref, out_ref, kb_ref, vt_ref, tail_ref, acc_ref, *, seq):
    qi = pl.program_id(2)
    tq = SB_TQ
    t0 = qi * tq
    scale = HEAD_DIM ** -0.5

    kb = SB_KB

    @pl.when(qi == 0)
    def _():
        _fill_kv_scratch(k_ref, v_ref, kb_ref, vt_ref, seq, kb)

    q = q_ref[...]
    t_lane = t0 + lax.broadcasted_iota(jnp.int32, (1, tq), 1)
    tail_ref[...] = jnp.zeros_like(tail_ref)
    acc_ref[...] = jnp.zeros_like(acc_ref)
    n_chunks = (t0 + tq) // kb

    def body(kk, carry):
        k0 = pl.multiple_of((n_chunks - 1 - kk) * kb, kb)
        z = _dot_nt(kb_ref[pl.ds(k0, kb), :], q) * scale
        valid = (k0 + lax.broadcasted_iota(jnp.int32, (kb, 1), 0)) < t_lane
        sp = _softplus(z)
        hi, lo = _split(jnp.where(valid, -sp, 0.0))
        sums = _dot(lw_ref[...], hi) + _dot(lw_ref[...], lo)
        tail = tail_ref[...]
        a = jnp.where(valid, jnp.exp(z - sp + sums[:kb] + tail), 0.0)
        acc_ref[...] += _dot(vt_ref[:, pl.ds(k0, kb)], a.astype(MXU_DTYPE))
        tail_ref[...] = tail + sums[kb:kb + 1]
        return carry

    lax.fori_loop(0, n_chunks, body, 0)
    out_ref[...] = acc_ref[...].T


def _sb_prompt(sbq, sbkv, batch, seq):
    m = sbq.shape[0]
    assert seq % SB_TQ == 0 and SB_TQ % SB_KB == 0
    nq = seq // SB_TQ
    lw = _suffix_matrix_rows(SB_KB)
    return pl.pallas_call(
        functools.partial(_sb_prompt_kernel, seq=seq),
        grid=(batch, SB_HEADS, nq),
        in_specs=[pl.BlockSpec((SB_TQ, HEAD_DIM), lambda b, h, q: (b * nq + q, h)),
                  pl.BlockSpec((seq, HEAD_DIM), lambda b, h, q: (b, h)),
                  pl.BlockSpec((seq, HEAD_DIM), lambda b, h, q: (b, SB_HEADS + h)),
                  pl.BlockSpec(lw.shape, lambda b, h, q: (0, 0))],
        out_specs=pl.BlockSpec((SB_TQ, HEAD_DIM), lambda b, h, q: (b * nq + q, h)),
        out_shape=jax.ShapeDtypeStruct((m, SB_HEADS * HEAD_DIM), F32),
        scratch_shapes=[pltpu.VMEM((seq, HEAD_DIM), MXU_DTYPE), pltpu.VMEM((HEAD_DIM, seq), MXU_DTYPE),
                        pltpu.VMEM((1, SB_TQ), F32), pltpu.VMEM((HEAD_DIM, SB_TQ), F32)],
        compiler_params=_params("arbitrary", "arbitrary", "arbitrary"),
        name="sb_prompt",
    )(sbq, sbkv, sbkv, lw)


DIFF_TQ = 512
DIFF_KB = 256


def _diff_lambda(lam_ref, lam_init):
    lv = lam_ref[...]
    s1 = jnp.sum(lv[0:1] * lv[1:2], axis=1, keepdims=True)
    s2 = jnp.sum(lv[2:3] * lv[3:4], axis=1, keepdims=True)
    return jnp.exp(s1) - jnp.exp(s2) + lam_init


def _diff_finish(o0, o1, lam, g, lam_init):
    o = o0 - lam * o1
    o = o * lax.rsqrt(jnp.mean(o * o, axis=-1, keepdims=True) + LN_EPS)
    return o * g * (1.0 - lam_init)


def _diff_prompt_kernel(q_ref, k_ref, v_ref, lam_ref, g_ref, out_ref, kb_ref, vt_ref, m_ref, l_ref, acc_ref,
                        *, lam_init, seq):
    qi = pl.program_id(2)
    tq = DIFF_TQ
    t0 = qi * tq
    scale = DIFF_DIM ** -0.5

    kb = DIFF_KB

    @pl.when(qi == 0)
    def _():
        _fill_kv_scratch(k_ref, v_ref, kb_ref, vt_ref, seq, kb)

    q = q_ref[...]
    lane = lax.broadcasted_iota(jnp.int32, (tq, HEAD_DIM), 1)
    zero = jnp.zeros_like(q)
    qq = jnp.concatenate([jnp.where(lane < DIFF_DIM, q, zero), jnp.where(lane >= DIFF_DIM, q, zero)], axis=0)
    t_q = t0 + lax.broadcasted_iota(jnp.int32, (1, tq), 1)
    t_lane = jnp.concatenate([t_q, t_q], axis=1)
    m_ref[...] = jnp.full_like(m_ref, NEG_INF)
    l_ref[...] = jnp.zeros_like(l_ref)
    acc_ref[...] = jnp.zeros_like(acc_ref)

    def body(kj, carry):
        k0 = pl.multiple_of(kj * kb, kb)
        s = _dot_nt(kb_ref[pl.ds(k0, kb), :], qq) * scale
        kpos = k0 + lax.broadcasted_iota(jnp.int32, (kb, 1), 0)
        _online_softmax_step(jnp.where(kpos <= t_lane, s, NEG_INF), vt_ref[:, pl.ds(k0, kb)],
                             m_ref, l_ref, acc_ref)
        return carry

    lax.fori_loop(0, (t0 + tq) // kb, body, 0)
    o = acc_ref[...] / l_ref[...]
    lam = _diff_lambda(lam_ref, lam_init)
    out_ref[...] = _diff_finish(o[:, :tq].T, o[:, tq:].T, lam, g_ref[...], lam_init)


def _diff_prompt(dq, dkv, lam_rows, norm_g, lam_init, batch, seq):
    m = dq.shape[0]
    assert seq % DIFF_TQ == 0 and DIFF_TQ % DIFF_KB == 0
    nq = seq // DIFF_TQ
    return pl.pallas_call(
        functools.partial(_diff_prompt_kernel, lam_init=lam_init, seq=seq),
        grid=(batch, DIFF_HEADS, nq),
        in_specs=[pl.BlockSpec((DIFF_TQ, HEAD_DIM), lambda b, h, q: (b * nq + q, h)),
                  pl.BlockSpec((seq, HEAD_DIM), lambda b, h, q: (b, h)),
                  pl.BlockSpec((seq, HEAD_DIM), lambda b, h, q: (b, DIFF_HEADS + h)),
                  pl.BlockSpec(lam_rows.shape, lambda b, h, q: (0, 0)),
                  pl.BlockSpec((1, HEAD_DIM), lambda b, h, q: (0, 0))],
        out_specs=pl.BlockSpec((DIFF_TQ, HEAD_DIM), lambda b, h, q: (b * nq + q, h)),
        out_shape=jax.ShapeDtypeStruct((m, DIFF_HEADS * HEAD_DIM), F32),
        scratch_shapes=[pltpu.VMEM((seq, HEAD_DIM), MXU_DTYPE), pltpu.VMEM((HEAD_DIM, seq), MXU_DTYPE),
                        pltpu.VMEM((1, 2 * DIFF_TQ), F32), pltpu.VMEM((1, 2 * DIFF_TQ), F32),
                        pltpu.VMEM((HEAD_DIM, 2 * DIFF_TQ), F32)],
        compiler_params=_params("arbitrary", "arbitrary", "arbitrary"),
        name="diff_prompt",
    )(dq, dkv, dkv, lam_rows, norm_g.reshape(1, HEAD_DIM))


def _merge_kernel(on_ref, os_ref, od_ref, mg_ref, x_ref, wo_ref, g_ref, b_ref, y_ref, yb_ref, *, alpha):
    n1 = on_ref.shape[1]
    n2 = n1 + os_ref.shape[1]
    r = _dot((on_ref[...] * mg_ref[:, :n1]).astype(MXU_DTYPE), wo_ref[:n1, :])
    r = r + _dot((os_ref[...] * mg_ref[:, n1:n2]).astype(MXU_DTYPE), wo_ref[n1:n2, :])
    r = r + _dot((od_ref[...] * mg_ref[:, n2:]).astype(MXU_DTYPE), wo_ref[n2:, :])
    y = _layer_norm(alpha * x_ref[...] + r, g_ref[...], b_ref[...])
    y_ref[...] = y
    yb_ref[...] = y.astype(yb_ref.dtype)


def _merge(o_nsa, o_sb, o_diff, mg, x, wo, g, b, alpha, tm):
    m, d = x.shape

    def row(width):
        return pl.BlockSpec((tm, width), lambda i: (i, 0))

    def full(a):
        return pl.BlockSpec(a.shape, lambda i: (0, 0))

    g2, b2 = g.reshape(1, d), b.reshape(1, d)
    return pl.pallas_call(
        functools.partial(_merge_kernel, alpha=alpha),
        grid=(m // tm,),
        in_specs=[row(o_nsa.shape[1]), row(o_sb.shape[1]), row(o_diff.shape[1]), row(mg.shape[1]), row(d),
                  full(wo), full(g2), full(b2)],
        out_specs=[row(d), row(d)],
        out_shape=[jax.ShapeDtypeStruct((m, d), F32), jax.ShapeDtypeStruct((m, d), MXU_DTYPE)],
        compiler_params=_params("arbitrary"),
        name="merge_out_proj",
    )(o_nsa, o_sb, o_diff, mg, x, wo, g2, b2)


FFN_TF = 1024


def _ffn_kernel(x_ref, xb_ref, wu_ref, wd_ref, g_ref, b_ref, y_ref, yb_ref, acc_ref, *, alpha):
    j = pl.program_id(1)

    @pl.when(j == 0)
    def _():
        acc_ref[...] = jnp.zeros_like(acc_ref)

    h = jnp.maximum(_dot(xb_ref[...], wu_ref[...]), 0.0)
    acc_ref[...] += _dot((h * h).astype(MXU_DTYPE), wd_ref[...])

    @pl.when(j == pl.num_programs(1) - 1)
    def _():
        y = _layer_norm(alpha * x_ref[...] + acc_ref[...], g_ref[...], b_ref[...])
        y_ref[...] = y
        yb_ref[...] = y.astype(yb_ref.dtype)


def _ffn(x, xb, wu, wd, g, b, alpha, tm):
    m, d = x.shape
    dff = wu.shape[1]
    g2, b2 = g.reshape(1, d), b.reshape(1, d)
    return pl.pallas_call(
        functools.partial(_ffn_kernel, alpha=alpha),
        grid=(m // tm, dff // FFN_TF),
        in_specs=[pl.BlockSpec((tm, d), lambda i, j: (i, 0)),
                  pl.BlockSpec((tm, d), lambda i, j: (i, 0)),
                  pl.BlockSpec((d, FFN_TF), lambda i, j: (0, j)),
                  pl.BlockSpec((FFN_TF, d), lambda i, j: (j, 0)),
                  pl.BlockSpec((1, d), lambda i, j: (0, 0)),
                  pl.BlockSpec((1, d), lambda i, j: (0, 0))],
        out_specs=[pl.BlockSpec((tm, d), lambda i, j: (i, 0)), pl.BlockSpec((tm, d), lambda i, j: (i, 0))],
        out_shape=[jax.ShapeDtypeStruct((m, d), F32), jax.ShapeDtypeStruct((m, d), MXU_DTYPE)],
        scratch_shapes=[pltpu.VMEM((tm, d), F32)],
        compiler_params=_params("arbitrary", "arbitrary"),
        name="ffn",
    )(x, xb, wu, wd, g2, b2)


def _expand_matrix(n_groups, width):
    c = np.arange(LANES).reshape(LANES, 1)
    j = np.arange(n_groups * width).reshape(1, -1) // width
    return jnp.asarray(c == j, dtype=MXU_DTYPE)


def _block_diag_q(q, n_heads, width, scale):
    b = q.shape[0]
    head = jnp.arange(n_heads * width) // width
    onehot = (head[:, None] == jnp.arange(LANES)[None, :]).astype(F32)
    return ((q.astype(F32) * scale)[:, :, None] * onehot[None]).astype(MXU_DTYPE)


def _select_kernel(qsel_ref, kcv_ref, e2_ref, oc_ref, idx_ref, *, pos_q, n_rows):
    half = KV_WIDTH
    kcv = kcv_ref[0]
    kc = kcv[:, :half].astype(MXU_DTYPE)
    vc = kcv[:, half:]
    n_idx = lax.broadcasted_iota(jnp.int32, (n_rows, LANES), 0)
    vis = ((n_idx + 1) * NSA_BLOCK - 1) <= pos_q
    imp = jnp.zeros((n_rows, LANES), F32)
    for g in range(NSA_GROUP):
        s = _dot(kc, qsel_ref[0, g])
        xm = jnp.where(vis, s, NEG_INF)
        mx = jnp.max(xm, axis=0, keepdims=True)
        e = jnp.where(vis, jnp.exp(xm - mx), 0.0)
        den = jnp.sum(e, axis=0, keepdims=True)
        p = e / jnp.where(den > 0.0, den, 1.0)
        imp = imp + p
        o = jnp.sum(_dot2(p, e2_ref[...]) * vc, axis=0, keepdims=True)
        for h in range(NSA_KV_HEADS):
            c = (h * NSA_GROUP + g) * HEAD_DIM
            oc_ref[0, :, c:c + HEAD_DIM] = o[:, h * HEAD_DIM:(h + 1) * HEAD_DIM]
    cur = pos_q // NSA_BLOCK
    n_f = n_idx.astype(F32)
    work = jnp.where(n_idx < cur, imp, -1.0)
    for k in range(NSA_TOPN - 1):
        mx = jnp.max(work, axis=0, keepdims=True)
        ix = jnp.min(jnp.where(work == mx, n_f, float(n_rows)), axis=0, keepdims=True)
        idx_ref[0, k:k + 1, :] = ix.astype(jnp.int32)
        work = jnp.where(n_f == ix, -2.0, work)
    idx_ref[0, NSA_TOPN - 1:NSA_TOPN, :] = jnp.full((1, LANES), cur, jnp.int32)


def _select(qsel, kcv, pos_q):
    nb, n_rows, width = kcv.shape
    e2 = _expand_matrix(NSA_KV_HEADS, HEAD_DIM)
    return pl.pallas_call(
        functools.partial(_select_kernel, pos_q=pos_q, n_rows=n_rows),
        grid=(nb,),
        in_specs=[pl.BlockSpec((1,) + qsel.shape[1:], lambda b: (b, 0, 0, 0)),
                  pl.BlockSpec((1, n_rows, width), lambda b: (b, 0, 0)),
                  pl.BlockSpec(e2.shape, lambda b: (0, 0))],
        out_specs=[pl.BlockSpec((1, 1, NSA_HEADS * HEAD_DIM), lambda b: (b, 0, 0)),
                   pl.BlockSpec((1, NSA_TOPN, LANES), lambda b: (b, 0, 0))],
        out_shape=[jax.ShapeDtypeStruct((nb, 1, NSA_HEADS * HEAD_DIM), F32),
                   jax.ShapeDtypeStruct((nb, NSA_TOPN, LANES), jnp.int32)],
        compiler_params=_params("arbitrary"),
        name="nsa_select",
    )(qsel, kcv, e2)


def _softmax_rows_plus_one(s, vis, s_new):
    xm = s if vis is None else jnp.where(vis, s, NEG_INF)
    mx = jnp.maximum(jnp.max(xm, axis=0, keepdims=True), s_new)
    e = jnp.exp(xm - mx)
    if vis is not None:
        e = jnp.where(vis, e, 0.0)
    e_new = jnp.exp(s_new - mx)
    den = jnp.sum(e, axis=0, keepdims=True) + e_new
    return e / den, e_new / den


def _nsa_sample_kernel(idx_ref, pt_ref, q_ref, oc_ref, slc_new_ref, win_new_ref, ws_ref, ng_ref, e4_ref,
                       cache_ref, out_ref, kvbuf, sem, *, layer, npg, wb):
    b = pl.program_id(0)
    n_old = NSA_TOPN - 1
    blk_rows = NSA_BLOCK * SLABS
    blocks_per_page = cache_ref.shape[2] // blk_rows

    def copies(fn):
        for h in range(NSA_KV_HEADS):
            for k in range(n_old):
                n = idx_ref[(b * NSA_TOPN + k) * NSA_KV_HEADS + h]
                page = pt_ref[b * npg + n // blocks_per_page]
                r0 = pl.multiple_of((n % blocks_per_page) * blk_rows, blk_rows)
                src = cache_ref.at[layer, page, pl.ds(r0, blk_rows), :]
                fn(pltpu.make_async_copy(src, kvbuf.at[h, pl.ds(k * blk_rows, blk_rows), :], sem.at[0]))

    copies(lambda cp: cp.start())
    copies(lambda cp: cp.wait())

    gw = NSA_GROUP * HEAD_DIM
    e4 = e4_ref[...]
    slc_new = slc_new_ref[0]
    win_new = win_new_ref[0]
    gates = ng_ref[0]
    n_keys = n_old * NSA_BLOCK
    i_w = lax.broadcasted_iota(jnp.int32, (wb, LANES), 0)
    vis_w = (wb - i_w) < WINDOW

    def attend(keys, vals, vis, k_new, v_new, qg):
        s = _dot(keys.astype(MXU_DTYPE), qg)
        s_new = _dot(jnp.concatenate([k_new] * 8, axis=0).astype(MXU_DTYPE), qg)[0:1]
        p, p_new = _softmax_rows_plus_one(s, vis, s_new)
        vt = jnp.concatenate([vals] * NSA_GROUP, axis=1)
        o = jnp.sum(_dot2(p, e4) * vt, axis=0, keepdims=True)
        o_new = _dot2(jnp.concatenate([p_new] * 8, axis=0), e4)[0:1] * jnp.concatenate([v_new] * NSA_GROUP, axis=1)
        return o + o_new

    for h in range(NSA_KV_HEADS):
        qg = q_ref[0, h]
        ks, vs = h * HEAD_DIM, KV_WIDTH + h * HEAD_DIM
        o_s = attend(_slab_rows(kvbuf.at[h], 0, n_keys, SLABS, [h]),
                     _slab_rows(kvbuf.at[h], 0, n_keys, SLABS, [NSA_KV_HEADS + h]), None,
                     slc_new[:, ks:ks + HEAD_DIM], slc_new[:, vs:vs + HEAD_DIM], qg)
        o_w = attend(_slab_rows(ws_ref.at[0], 0, wb, SLABS, [h]),
                     _slab_rows(ws_ref.at[0], 0, wb, SLABS, [NSA_KV_HEADS + h]), vis_w,
                     win_new[:, ks:ks + HEAD_DIM], win_new[:, vs:vs + HEAD_DIM], qg)
        o_c = oc_ref[0, :, h * gw:(h + 1) * gw]
        for g in range(NSA_GROUP):
            c = h * LANES + 3 * g
            sl = slice(g * HEAD_DIM, (g + 1) * HEAD_DIM)
            out_ref[0, :, h * gw + g * HEAD_DIM:h * gw + (g + 1) * HEAD_DIM] = (
                gates[:, c:c + 1] * o_c[:, sl] + gates[:, c + 1:c + 2] * o_s[:, sl]
                + gates[:, c + 2:c + 3] * o_w[:, sl])


def _nsa_sample(idx, table, qslc, o_c, slc_new, win_new, win_state, ng, cache, layer):
    nb, npg = table.shape
    wb = win_state.shape[2] // SLABS
    width = 2 * KV_WIDTH
    e4 = _expand_matrix(NSA_GROUP, HEAD_DIM)
    ws = win_state.reshape(win_state.shape[0] * nb, wb * SLABS, HEAD_DIM)
    idx_flat = idx[:, :, :NSA_KV_HEADS].reshape(-1)

    def one(width_):
        return pl.BlockSpec((1, 1, width_), lambda b, ix, pt: (b, 0, 0))

    return pl.pallas_call(
        functools.partial(_nsa_sample_kernel, layer=layer, npg=npg, wb=wb),
        grid_spec=pltpu.PrefetchScalarGridSpec(
            num_scalar_prefetch=2,
            grid=(nb,),
            in_specs=[pl.BlockSpec((1,) + qslc.shape[1:], lambda b, ix, pt: (b, 0, 0, 0)),
                      one(NSA_HEADS * HEAD_DIM), one(width), one(width),
                      pl.BlockSpec((1, wb * SLABS, HEAD_DIM), lambda b, ix, pt: (layer * nb + b, 0, 0)),
                      one(COL_TILE),
                      pl.BlockSpec(e4.shape, lambda b, ix, pt: (0, 0)),
                      pl.BlockSpec(memory_space=pl.ANY)],
            out_specs=one(NSA_HEADS * HEAD_DIM),
            scratch_shapes=[pltpu.VMEM((NSA_KV_HEADS, (NSA_TOPN - 1) * NSA_BLOCK * SLABS, HEAD_DIM), F32),
                            pltpu.SemaphoreType.DMA((1,))]),
        out_shape=jax.ShapeDtypeStruct((nb, 1, NSA_HEADS * HEAD_DIM), F32),
        compiler_params=_params("arbitrary"),
        name="nsa_sample",
    )(idx_flat, table.reshape(-1), qslc, o_c, slc_new, win_new, ws, ng, e4, cache)


PAGES_PER_STEP = 16
SB_DEC_KB = 256


def _page_pipeline(pt_ref, cache_ref, buf, sem, layer, npg, n_chunks, reverse):
    b = pl.program_id(0)
    c = pl.program_id(1)
    step = b * n_chunks + c
    slot = step % 2
    rpp = cache_ref.shape[2]

    def copies(bb, cc, sl, fn):
        first = (n_chunks - 1 - cc if reverse else cc) * PAGES_PER_STEP
        for i in range(PAGES_PER_STEP):
            page = pt_ref[bb * npg + first + i]
            fn(pltpu.make_async_copy(cache_ref.at[layer, page], buf.at[sl, pl.ds(i * rpp, rpp), :], sem.at[sl]))

    @pl.when(step == 0)
    def _():
        copies(b, c, slot, lambda cp: cp.start())

    @pl.when(step + 1 < pl.num_programs(0) * n_chunks)
    def _():
        nxt = step + 1
        copies(nxt // n_chunks, nxt % n_chunks, 1 - slot, lambda cp: cp.start())

    copies(b, c, slot, lambda cp: cp.wait())
    return slot


def _sb_sample_kernel(pt_ref, q_ref, lw_ref, e_ref, cache_ref, out_ref, buf, sem, tail_ref, acc_ref,
                      *, layer, npg, n_chunks):
    c = pl.program_id(1)
    slot = _page_pipeline(pt_ref, cache_ref, buf, sem, layer, npg, n_chunks, reverse=True)
    kw = SB_HEADS * HEAD_DIM
    kb = SB_DEC_KB

    @pl.when(c == 0)
    def _():
        tail_ref[...] = jnp.zeros_like(tail_ref)
        acc_ref[...] = jnp.zeros_like(acc_ref)

    qbd = q_ref[0]
    slabs = 2 * SB_HEADS
    n_keys = buf.shape[1] // slabs
    for j in reversed(range(n_keys // kb)):
        keys = _slab_rows(buf.at[slot], j * kb * slabs, kb, slabs, range(SB_HEADS))
        vals = _slab_rows(buf.at[slot], j * kb * slabs, kb, slabs, range(SB_HEADS, slabs))
        z = _dot(keys.astype(MXU_DTYPE), qbd)
        sp = _softplus(z)
        hi, lo = _split(-sp)
        suffix = _dot(lw_ref[...], hi) + _dot(lw_ref[...], lo)
        tail = tail_ref[0:1, :]
        a = jnp.exp(z - sp + suffix[:kb] + tail)
        contrib = _dot(a.astype(MXU_DTYPE), e_ref[...]) * vals
        acc_ref[...] += jnp.sum(contrib.reshape(kb // 8, 8, kw), axis=0)
        tail_ref[...] = tail_ref[...] + suffix[kb:kb + 8]

    @pl.when(c == n_chunks - 1)
    def _():
        out_ref[0] = jnp.sum(acc_ref[...], axis=0, keepdims=True)


def _sb_sample(table, qbd, cache, layer):
    nb, npg = table.shape
    flat_rows = cache.shape[2]
    rpp = flat_rows // (2 * SB_HEADS)
    assert npg % PAGES_PER_STEP == 0 and (PAGES_PER_STEP * rpp) % SB_DEC_KB == 0
    n_chunks = npg // PAGES_PER_STEP
    lw = _suffix_matrix_rows(SB_DEC_KB)
    e = _expand_matrix(SB_HEADS, HEAD_DIM)
    kw = SB_HEADS * HEAD_DIM
    return pl.pallas_call(
        functools.partial(_sb_sample_kernel, layer=layer, npg=npg, n_chunks=n_chunks),
        grid_spec=pltpu.PrefetchScalarGridSpec(
            num_scalar_prefetch=1,
            grid=(nb, n_chunks),
            in_specs=[pl.BlockSpec((1,) + qbd.shape[1:], lambda b, c, pt: (b, 0, 0)),
                      pl.BlockSpec(lw.shape, lambda b, c, pt: (0, 0)),
                      pl.BlockSpec(e.shape, lambda b, c, pt: (0, 0)),
                      pl.BlockSpec(memory_space=pl.ANY)],
            out_specs=pl.BlockSpec((1, 1, kw), lambda b, c, pt: (b, 0, 0)),
            scratch_shapes=[pltpu.VMEM((2, PAGES_PER_STEP * flat_rows, HEAD_DIM), F32),
                            pltpu.SemaphoreType.DMA((2,)),
                            pltpu.VMEM((8, LANES), F32),
                            pltpu.VMEM((8, kw), F32)]),
        out_shape=jax.ShapeDtypeStruct((nb, 1, kw), F32),
        compiler_params=_params("arbitrary", "arbitrary"),
        name="sb_sample",
    )(table.reshape(-1), qbd, lw, e, cache)


DIFF_DEC_KB = 512


def _diff_sample_kernel(pt_ref, q_ref, e0_ref, e1_ref, new_ref, lam_ref, g_ref, cache_ref, out_ref,
                        buf, sem, m_ref, l_ref, acc0_ref, acc1_ref, *, layer, npg, n_chunks, lam_init):
    c = pl.program_id(1)
    slot = _page_pipeline(pt_ref, cache_ref, buf, sem, layer, npg, n_chunks, reverse=False)
    kw = DIFF_HEADS * HEAD_DIM
    kb = DIFF_DEC_KB

    @pl.when(c == 0)
    def _():
        m_ref[...] = jnp.full_like(m_ref, NEG_INF)
        l_ref[...] = jnp.zeros_like(l_ref)
        acc0_ref[...] = jnp.zeros_like(acc0_ref)
        acc1_ref[...] = jnp.zeros_like(acc1_ref)

    qbd = q_ref[0]

    def absorb(keys, v, vis):
        n = keys.shape[0]
        s = _dot(keys.astype(MXU_DTYPE), qbd)
        if vis is not None:
            s = jnp.where(vis, s, NEG_INF)
        m_old = m_ref[...]
        m_new = jnp.maximum(m_old, jnp.max(s, axis=0, keepdims=True))
        alpha = jnp.exp(m_old - m_new)
        p = jnp.exp(s - m_new)
        if vis is not None:
            p = jnp.where(vis, p, 0.0)
        l_ref[...] = alpha * l_ref[...] + jnp.sum(p, axis=0, keepdims=True)
        m_ref[...] = m_new
        pb = p.astype(MXU_DTYPE)
        a8 = jnp.concatenate([alpha] * 8, axis=0)
        for e_ref, acc_ref in ((e0_ref, acc0_ref), (e1_ref, acc1_ref)):
            pe = _dot(pb, e_ref[...])
            part = jnp.sum((pe * v).reshape(n // 8, 8, kw), axis=0)
            acc_ref[...] = acc_ref[...] * _dot2(a8, e_ref[...]) + part

    slabs = 2 * DIFF_HEADS
    n_keys = buf.shape[1] // slabs
    for j in range(n_keys // kb):
        absorb(_slab_rows(buf.at[slot], j * kb * slabs, kb, slabs, range(DIFF_HEADS)),
               _slab_rows(buf.at[slot], j * kb * slabs, kb, slabs, range(DIFF_HEADS, slabs)), None)

    @pl.when(c == n_chunks - 1)
    def _():
        new8 = jnp.concatenate([new_ref[0]] * 8, axis=0)
        first = lax.broadcasted_iota(jnp.int32, (8, LANES), 0) == 0
        absorb(new8[:, :kw], new8[:, kw:], first)
        l8 = jnp.concatenate([l_ref[...]] * 8, axis=0)
        o0 = jnp.sum(acc0_ref[...], axis=0, keepdims=True) / _dot2(l8, e0_ref[...])[0:1]
        o1 = jnp.sum(acc1_ref[...], axis=0, keepdims=True) / _dot2(l8, e1_ref[...])[0:1]
        lam = _diff_lambda(lam_ref, lam_init)
        for h in range(DIFF_HEADS):
            sl = slice(h * HEAD_DIM, (h + 1) * HEAD_DIM)
            out_ref[0, :, sl] = _diff_finish(o0[:, sl], o1[:, sl], lam, g_ref[...], lam_init)


def _diff_sample(table, qbd, new_rows, lam_rows, norm_g, cache, layer, lam_init):
    nb, npg = table.shape
    flat_rows = cache.shape[2]
    rpp = flat_rows // (2 * DIFF_HEADS)
    width = new_rows.shape[2]
    assert npg % PAGES_PER_STEP == 0 and (PAGES_PER_STEP * rpp) % DIFF_DEC_KB == 0
    n_chunks = npg // PAGES_PER_STEP
    kw = DIFF_HEADS * HEAD_DIM
    both = np.arange(LANES).reshape(LANES, 1)
    col_head = np.arange(kw).reshape(1, kw) // HEAD_DIM
    e0 = jnp.asarray(both == 2 * col_head, dtype=MXU_DTYPE)
    e1 = jnp.asarray(both == 2 * col_head + 1, dtype=MXU_DTYPE)
    return pl.pallas_call(
        functools.partial(_diff_sample_kernel, layer=layer, npg=npg, n_chunks=n_chunks, lam_init=lam_init),
        grid_spec=pltpu.PrefetchScalarGridSpec(
            num_scalar_prefetch=1,
            grid=(nb, n_chunks),
            in_specs=[pl.BlockSpec((1,) + qbd.shape[1:], lambda b, c, pt: (b, 0, 0)),
                      pl.BlockSpec(e0.shape, lambda b, c, pt: (0, 0)),
                      pl.BlockSpec(e1.shape, lambda b, c, pt: (0, 0)),
                      pl.BlockSpec((1, 1, width), lambda b, c, pt: (b, 0, 0)),
                      pl.BlockSpec(lam_rows.shape, lambda b, c, pt: (0, 0)),
                      pl.BlockSpec((1, HEAD_DIM), lambda b, c, pt: (0, 0)),
                      pl.BlockSpec(memory_space=pl.ANY)],
            out_specs=pl.BlockSpec((1, 1, kw), lambda b, c, pt: (b, 0, 0)),
            scratch_shapes=[pltpu.VMEM((2, PAGES_PER_STEP * flat_rows, HEAD_DIM), F32),
                            pltpu.SemaphoreType.DMA((2,)),
                            pltpu.VMEM((1, LANES), F32), pltpu.VMEM((1, LANES), F32),
                            pltpu.VMEM((8, kw), F32), pltpu.VMEM((8, kw), F32)]),
        out_shape=jax.ShapeDtypeStruct((nb, 1, kw), F32),
        compiler_params=_params("arbitrary", "arbitrary"),
        name="diff_sample",
    )(table.reshape(-1), qbd, e0, e1, new_rows, lam_rows, norm_g.reshape(1, HEAD_DIM), cache)


SAMPLE_ROWS = 16


def _pick_tm(m, cap):
    tm = min(m, cap)
    while m % tm:
        tm //= 2
    return tm


def kernel(x_prompt, x_sample, cache_nsa_cmp_kv, cache_nsa_slc_kv, state_nsa_win_kv, cache_sb_kv,
           cache_diff_kv, page_table, w_in, w_o, ln1_g, ln1_b, ln2_g, ln2_b, w_up, w_down,
           nsa_cmp_pos, nsa_cmp_w1, nsa_cmp_w2, diff_lambda, diff_norm_g):
    depth = w_in.shape[0]
    batch, seq, d_model = x_prompt.shape
    nbs, dec_seq, _ = x_sample.shape
    assert dec_seq == 1 and nbs <= SAMPLE_ROWS
    npg = page_table.shape[1]
    n_pool, page = cache_nsa_cmp_kv.shape[1], cache_nsa_cmp_kv.shape[2]
    past = npg * page
    alpha = (2 * depth) ** 0.25
    width = 2 * KV_WIDTH
    mp = batch * seq
    tm_p = _pick_tm(seq, 1024)

    tabs_p = (_rope_tables(jnp.arange(seq), HEAD_DIM, HEAD_DIM // ROT_FRACTION)
              + _rope_tables(jnp.arange(seq), DIFF_DIM, DIFF_DIM // ROT_FRACTION))
    pos_s = jnp.full((SAMPLE_ROWS,), past, jnp.int32)
    tabs_s = (_rope_tables(pos_s, HEAD_DIM, HEAD_DIM // ROT_FRACTION)
              + _rope_tables(pos_s, DIFF_DIM, DIFF_DIM // ROT_FRACTION))

    cmp_cache = cache_nsa_cmp_kv.reshape(depth, n_pool, page * SLABS, HEAD_DIM)
    slc_cache = cache_nsa_slc_kv.reshape(depth, n_pool, page * SLABS, HEAD_DIM)
    sb_cache = cache_sb_kv.reshape(depth, n_pool, page * 2 * SB_HEADS, HEAD_DIM)
    diff_cache = cache_diff_kv.reshape(depth, n_pool, page * 2 * DIFF_HEADS, HEAD_DIM)
    win_state = state_nsa_win_kv.reshape(depth, nbs, state_nsa_win_kv.shape[2] * SLABS, HEAD_DIM)
    prompt_pages = jnp.arange(mp // page, dtype=jnp.int32).reshape(1, mp // page)
    new_pages = jnp.arange(nbs, dtype=jnp.int32).reshape(1, nbs)

    xp = x_prompt.reshape(mp, d_model)
    xs = jnp.pad(x_sample.reshape(nbs, d_model), ((0, SAMPLE_ROWS - nbs), (0, 0)))
    xp_b, xs_b = xp.astype(MXU_DTYPE), xs.astype(MXU_DTYPE)
    st_p = [[] for _ in range(5)]
    st_s = [[] for _ in range(5)]

    for l in range(depth):
        lam_init = 0.8 - 0.6 * math.exp(-0.3 * l)
        w_p = _permute_w_in(w_in[l])
        wo_b, wu_b, wd_b = w_o[l].astype(MXU_DTYPE), w_up[l].astype(MXU_DTYPE), w_down[l].astype(MXU_DTYPE)
        posc, w1bd, w2bd = _compress_weights(nsa_cmp_pos[l], nsa_cmp_w1[l], nsa_cmp_w2[l])
        lam_rows = diff_lambda[l].astype(F32)

        (qraw, qrot, cmp, slc, win, sbq, sbkv, dq, dkv, mg, ng) = _project(xp_b, w_p, tabs_p, tm_p, seq)
        kcv = _compress(cmp.reshape(1, mp // page, page * SLABS, HEAD_DIM), 0, prompt_pages, posc, w1bd, w2bd)
        o_nsa = _nsa_prompt(qraw, qrot, kcv, slc, win, ng, batch, seq)
        o_sb = _sb_prompt(sbq, sbkv, batch, seq)
        o_diff = _diff_prompt(dq, dkv, lam_rows, diff_norm_g[l], lam_init, batch, seq)
        x1, x1b = _merge(o_nsa, o_sb, o_diff, mg, xp, wo_b, ln1_g[l], ln1_b[l], alpha, _pick_tm(mp, 256))
        xp, xp_b = _ffn(x1, x1b, wu_b, wd_b, ln2_g[l], ln2_b[l], alpha, _pick_tm(mp, 512))
        win_rows = min(WINDOW, seq)
        st_p[0].append(cmp.reshape(batch, seq, 2, NSA_KV_HEADS, HEAD_DIM))
        st_p[1].append(slc.reshape(batch, seq, 2, NSA_KV_HEADS, HEAD_DIM))
        st_p[2].append(win.reshape(batch, seq, 2, NSA_KV_HEADS, HEAD_DIM)[:, seq - win_rows:])
        st_p[3].append(sbkv.reshape(batch, seq, 2, SB_HEADS, HEAD_DIM))
        st_p[4].append(dkv.reshape(batch, seq, 2, DIFF_HEADS, HEAD_DIM))

        (qraw, qrot, cmp, slc, win, sbq, sbkv, dq, dkv, mg, ng) = _project(
            xs_b, w_p, tabs_s, SAMPLE_ROWS, SAMPLE_ROWS)
        kcv_past = _compress(cmp_cache, l, page_table, posc, w1bd, w2bd)
        new_blk = jnp.zeros((1, nbs, NSA_BLOCK, width), F32).at[0, :, 0, :].set(cmp[:nbs])
        kcv_new = _compress(new_blk.reshape(1, nbs, NSA_BLOCK * SLABS, HEAD_DIM), 0, new_pages, posc, w1bd, w2bd)
        n_past = kcv_past.shape[1]
        n_rows = -(-(n_past + 1) // 8) * 8
        kcv_all = jnp.concatenate([kcv_past, kcv_new.reshape(nbs, 1, width),
                                   jnp.zeros((nbs, n_rows - n_past - 1, width), F32)], axis=1)
        scale = HEAD_DIM ** -0.5
        q4 = qraw[:nbs].reshape(nbs, NSA_KV_HEADS, NSA_GROUP, HEAD_DIM).transpose(0, 2, 1, 3)
        qsel = jnp.stack([_block_diag_q(q4[:, g].reshape(nbs, KV_WIDTH), NSA_KV_HEADS, HEAD_DIM, scale)
                          for g in range(NSA_GROUP)], axis=1)
        o_c, idx = _select(qsel, kcv_all, past)
        qr = qrot[:nbs].reshape(nbs, NSA_KV_HEADS, NSA_GROUP, HEAD_DIM).astype(F32) * scale
        qslc = jnp.pad(qr.transpose(0, 1, 3, 2), ((0, 0), (0, 0), (0, 0), (0, LANES - NSA_GROUP))).astype(MXU_DTYPE)
        o_nsa = _nsa_sample(idx, page_table, qslc, o_c, slc[:nbs].reshape(nbs, 1, width),
                            win[:nbs].reshape(nbs, 1, width), win_state, ng[:nbs].reshape(nbs, 1, COL_TILE),
                            slc_cache, l)
        o_sb = _sb_sample(page_table, _block_diag_q(sbq[:nbs], SB_HEADS, HEAD_DIM, scale), sb_cache, l)
        o_diff = _diff_sample(page_table, _block_diag_q(dq[:nbs], 2 * DIFF_HEADS, DIFF_DIM, DIFF_DIM ** -0.5),
                              dkv[:nbs].reshape(nbs, 1, -1), lam_rows, diff_norm_g[l], diff_cache, l, lam_init)

        def pad_rows(a):
            return jnp.pad(a.reshape(nbs, -1), ((0, SAMPLE_ROWS - nbs), (0, 0)))

        x1, x1b = _merge(pad_rows(o_nsa), pad_rows(o_sb), pad_rows(o_diff), mg, xs, wo_b, ln1_g[l], ln1_b[l],
                         alpha, SAMPLE_ROWS)
        xs, xs_b = _ffn(x1, x1b, wu_b, wd_b, ln2_g[l], ln2_b[l], alpha, SAMPLE_ROWS)
        new_win = win[:nbs].reshape(nbs, 1, 2, NSA_KV_HEADS, HEAD_DIM)
        wb = state_nsa_win_kv.shape[2]
        keep = min(WINDOW, past + 1)
        st_s[0].append(cmp[:nbs].reshape(nbs, 1, 2, NSA_KV_HEADS, HEAD_DIM))
        st_s[1].append(slc[:nbs].reshape(nbs, 1, 2, NSA_KV_HEADS, HEAD_DIM))
        st_s[2].append(jnp.concatenate([state_nsa_win_kv[l], new_win], axis=1)[:, wb + 1 - keep:])
        st_s[3].append(sbkv[:nbs].reshape(nbs, 1, 2, SB_HEADS, HEAD_DIM))
        st_s[4].append(dkv[:nbs].reshape(nbs, 1, 2, DIFF_HEADS, HEAD_DIM))

    return (xp.reshape(batch, seq, d_model), xs[:nbs].reshape(nbs, 1, d_model),
            jnp.stack(st_p[0]), jnp.stack(st_s[0]),
            jnp.stack(st_p[1]), jnp.stack(st_s[1]),
            jnp.stack(st_p[2]), jnp.stack(st_s[2]),
            jnp.stack(st_p[3]), jnp.stack(st_s[3]),
            jnp.stack(st_p[4]), jnp.stack(st_s[4]))
```

```python
import functools
import math

import numpy as np
import jax
import jax.numpy as jnp
from jax import lax
from jax.experimental import pallas as pl
from jax.experimental.pallas import tpu as pltpu

HEAD_DIM = 128
NSA_HEADS = 8
NSA_KV_HEADS = 2
NSA_GROUP = NSA_HEADS // NSA_KV_HEADS
NSA_BLOCK = 64
NSA_TOPN = 16
WINDOW = 512
SB_HEADS = 4
DIFF_HEADS = 4
DIFF_DIM = HEAD_DIM // 2
ROPE_THETA = 500000.0
ROT_FRACTION = 4
LN_EPS = 1e-5
NEG_INF = -1e30

F32 = jnp.float32
MXU_DTYPE = jnp.bfloat16

LANES = 128
COL_TILE = 256
VMEM_LIMIT_BYTES = 56 * 2**20

KV_WIDTH = NSA_KV_HEADS * HEAD_DIM
Q_TILES = NSA_HEADS * HEAD_DIM // COL_TILE
T_CMP = Q_TILES
T_SLC = T_CMP + 2
T_WIN = T_SLC + 2
T_SBQ = T_WIN + 2
T_SBKV = T_SBQ + SB_HEADS * HEAD_DIM // COL_TILE
T_DQ = T_SBKV + 2 * SB_HEADS * HEAD_DIM // COL_TILE
T_DKV = T_DQ + DIFF_HEADS * HEAD_DIM // COL_TILE
T_MG = T_DKV + 2 * DIFF_HEADS * HEAD_DIM // COL_TILE
MIX_WIDTH = (NSA_HEADS + SB_HEADS + DIFF_HEADS) * HEAD_DIM
T_NG = T_MG + MIX_WIDTH // COL_TILE
N_TILES = T_NG + 1


def _dot(a, b):
    return jnp.dot(a, b, preferred_element_type=F32)


def _dot_nt(a, b):
    return lax.dot_general(a, b, (((1,), (1,)), ((), ())), preferred_element_type=F32)


def _split(x):
    hi = x.astype(MXU_DTYPE)
    lo = (x - hi.astype(F32)).astype(MXU_DTYPE)
    return hi, lo


def _dot2(x, w):
    hi, lo = _split(x)
    return _dot(hi, w) + _dot(lo, w)


def _params(*sem):
    return pltpu.CompilerParams(dimension_semantics=sem, vmem_limit_bytes=VMEM_LIMIT_BYTES)


def _softplus(z):
    return jnp.maximum(z, 0.0) + jnp.log(1.0 + jnp.exp(-jnp.abs(z)))


def _layer_norm(h, g, b):
    mu = jnp.mean(h, axis=-1, keepdims=True)
    d = h - mu
    var = jnp.mean(d * d, axis=-1, keepdims=True)
    return d * lax.rsqrt(var + LN_EPS) * g + b


def _rope_tables(pos, width, rot_dim):
    half = rot_dim // 2
    inv_freq = ROPE_THETA ** (-jnp.arange(half, dtype=F32) / half)
    ang = pos.astype(F32)[:, None] * inv_freq[None, :]
    cos, sin = jnp.cos(ang), jnp.sin(ang)
    n = pos.shape[0]
    rest = width - 2 * half
    c = jnp.concatenate([cos, cos, jnp.ones((n, rest), F32)], axis=1)
    s_up = jnp.concatenate([jnp.zeros((n, half), F32), sin, jnp.zeros((n, rest), F32)], axis=1)
    s_dn = jnp.concatenate([-sin, jnp.zeros((n, half + rest), F32)], axis=1)
    rep = LANES // width
    c, s_up, s_dn = (jnp.tile(a, (1, rep)) for a in (c, s_up, s_dn))
    return (jnp.stack([c, jnp.ones_like(c)]), jnp.stack([s_up, jnp.zeros_like(c)]),
            jnp.stack([s_dn, jnp.zeros_like(c)]))


def _rope(a, c_ref, su_ref, sd_ref, shift):
    def one(h):
        return (h * c_ref[...] + pltpu.roll(h, shift, 1) * su_ref[...]
                + pltpu.roll(h, LANES - shift, 1) * sd_ref[...])
    return jnp.concatenate([one(a[:, :LANES]), one(a[:, LANES:])], axis=1)


def _put_slabs(ref, tile, val):
    rows = val.shape[0]
    slabs = ref.shape[0] // rows
    for k in range(COL_TILE // LANES):
        ref[pl.ds(2 * tile + k, rows, stride=slabs), :] = val[:, k * LANES:(k + 1) * LANES]


def _proj_kernel(x_ref, w_ref, *refs, kind, shift):
    acc = _dot(x_ref[...], w_ref[...])
    if kind == "q":
        c_ref, su_ref, sd_ref, raw_ref, rot_ref = refs
        raw_ref[...] = acc.astype(raw_ref.dtype)
        rot_ref[...] = _rope(acc, c_ref, su_ref, sd_ref, shift).astype(rot_ref.dtype)
    elif kind == "kv":
        _put_slabs(refs[0], pl.program_id(1), acc)
    elif kind == "kv_rope":
        c_ref, su_ref, sd_ref, out_ref = refs
        _put_slabs(out_ref, pl.program_id(1), _rope(acc, c_ref, su_ref, sd_ref, shift))
    elif kind == "cast":
        refs[0][...] = acc.astype(refs[0].dtype)
    elif kind == "cast_rope":
        c_ref, su_ref, sd_ref, out_ref = refs
        out_ref[...] = _rope(acc, c_ref, su_ref, sd_ref, shift).astype(out_ref.dtype)
    else:
        assert kind == "sigmoid"
        refs[0][...] = jax.nn.sigmoid(acc)


def _proj_group(kind, xb, w_p, tile0, n_tiles, tm, rows_per_seq, tabs=None, n_rope=0, shift=0,
                dtype=F32, slabs=0):
    m, d = xb.shape
    seq_tiles = rows_per_seq // tm
    in_specs = [pl.BlockSpec((tm, d), lambda i, n: (i, 0)),
                pl.BlockSpec((d, COL_TILE), lambda i, n: (0, tile0 + n))]
    args = [xb, w_p]
    if tabs is not None:
        tab_spec = pl.BlockSpec((None, tm, LANES),
                                lambda i, n: (jnp.where(n < n_rope, 0, 1), i % seq_tiles, 0))
        in_specs += [tab_spec] * 3
        args += list(tabs)
    if slabs:
        shape = jax.ShapeDtypeStruct((m * slabs, HEAD_DIM), F32)
        spec = pl.BlockSpec((tm * slabs, HEAD_DIM), lambda i, n: (i, 0))
    else:
        shape = jax.ShapeDtypeStruct((m, n_tiles * COL_TILE), dtype)
        spec = pl.BlockSpec((tm, COL_TILE), lambda i, n: (i, n))
    n_out = 2 if kind == "q" else 1
    out = pl.pallas_call(
        functools.partial(_proj_kernel, kind=kind, shift=shift),
        grid=(m // tm, n_tiles),
        in_specs=in_specs,
        out_specs=[spec] * n_out,
        out_shape=[shape] * n_out,
        compiler_params=_params("arbitrary", "arbitrary"),
        name="proj_" + kind,
    )(*args)
    return out if n_out > 1 else out[0]


def _project(xb, w_p, tabs, tm, rows_per_seq):
    head_tabs, diff_tabs = tabs[:3], tabs[3:]
    s_head = HEAD_DIM // ROT_FRACTION // 2
    s_diff = DIFF_DIM // ROT_FRACTION // 2

    def group(kind, tile0, n_tiles, **kw):
        return _proj_group(kind, xb, w_p, tile0, n_tiles, tm, rows_per_seq, **kw)

    qraw, qrot = group("q", 0, Q_TILES, tabs=head_tabs, n_rope=Q_TILES, shift=s_head, dtype=MXU_DTYPE)
    cmp = group("kv", T_CMP, 2, slabs=SLABS)
    slc = group("kv_rope", T_SLC, 2, tabs=head_tabs, n_rope=1, shift=s_head, slabs=SLABS)
    win = group("kv_rope", T_WIN, 2, tabs=head_tabs, n_rope=1, shift=s_head, slabs=SLABS)
    sbq = group("cast", T_SBQ, T_SBKV - T_SBQ, dtype=MXU_DTYPE)
    sbkv = group("kv", T_SBKV, T_DQ - T_SBKV, slabs=2 * SB_HEADS)
    n_dq = T_DKV - T_DQ
    dq = group("cast_rope", T_DQ, n_dq, tabs=diff_tabs, n_rope=n_dq, shift=s_diff, dtype=MXU_DTYPE)
    n_dkv = T_MG - T_DKV
    dkv = group("kv_rope", T_DKV, n_dkv, tabs=diff_tabs, n_rope=n_dkv // 2, shift=s_diff,
                slabs=2 * DIFF_HEADS)
    mg = group("sigmoid", T_MG, T_NG - T_MG)
    ng = group("sigmoid", T_NG, 1)
    return qraw, qrot, cmp, slc, win, sbq, sbkv, dq, dkv, mg, ng


def _permute_w_in(w):
    d = w.shape[0]
    g0 = Q_TILES * COL_TILE + 6 * KV_WIDTH
    ng = 3 * NSA_HEADS
    per = ng // NSA_KV_HEADS
    gate = w[:, g0:g0 + ng]
    pad = jnp.zeros((d, LANES - per), w.dtype)
    gate_tile = jnp.concatenate([gate[:, :per], pad, gate[:, per:], pad], axis=1)
    return jnp.concatenate([w[:, :g0], w[:, g0 + ng:], gate_tile], axis=1).astype(MXU_DTYPE)


R_CHUNK = 16
SLABS = 2 * NSA_KV_HEADS
SUBLANES = 8
BLK_SLOT = R_CHUNK * SLABS + SUBLANES


def _slab_rows(ref, first, count, stride, slabs):
    return jnp.concatenate([ref[pl.ds(first + c, count, stride=stride), :] for c in slabs], axis=1)


def _gelu(x):
    return 0.5 * x * (1.0 + jnp.tanh(math.sqrt(2.0 / math.pi) * (x + 0.044715 * (x * x * x))))


def _compress_kernel(pt_ref, pos_ref, w1_ref, w2_ref, cache_ref, out_ref, buf, sem, acc_ref,
                     *, layer, npg, bpp, n_steps):
    b = pl.program_id(0)
    j = pl.program_id(1)
    n_chunks = NSA_BLOCK // R_CHUNK
    step = b * n_chunks + j
    slot = step % 2
    nblk = npg * bpp

    def copy(bb, jj, sl, i, h):
        page = pt_ref[bb * npg + i]
        src = cache_ref.at[layer, page, pl.ds((h * NSA_BLOCK + jj * R_CHUNK) * SLABS, R_CHUNK * SLABS), :]
        dst = buf.at[sl, pl.ds((i * bpp + h) * BLK_SLOT, R_CHUNK * SLABS), :]
        return pltpu.make_async_copy(src, dst, sem.at[sl])

    def for_all(bb, jj, sl, start):
        def body(i, carry):
            for h in range(bpp):
                cp = copy(bb, jj, sl, i, h)
                if start:
                    cp.start(priority=h % 2)
                else:
                    cp.wait()
            return carry
        lax.fori_loop(0, npg, body, 0)

    @pl.when(step == 0)
    def _():
        for_all(b, j, slot, True)

    @pl.when(step + 1 < n_steps)
    def _():
        nxt = step + 1
        for_all(nxt // n_chunks, nxt % n_chunks, 1 - slot, True)

    for_all(b, j, slot, False)

    @pl.when(j == 0)
    def _():
        acc_ref[...] = jnp.zeros_like(acc_ref)

    half = 2 * HEAD_DIM
    a_k = jnp.zeros((nblk, half), F32)
    a_v = jnp.zeros((nblk, half), F32)
    for r in range(R_CHUNK):
        xr = _slab_rows(buf.at[slot], r * SLABS, nblk, BLK_SLOT, range(SLABS))
        xr = xr + pos_ref[pl.ds(j * R_CHUNK + r, 1), :]
        xb = xr.astype(MXU_DTYPE)
        a_k = a_k + _dot(xb[:, :half], w1_ref[r, 0])
        a_v = a_v + _dot(xb[:, half:], w1_ref[r, 1])
    acc_ref[:, :half] += a_k
    acc_ref[:, half:] += a_v

    @pl.when(j == n_chunks - 1)
    def _():
        h = _gelu(acc_ref[...]).astype(MXU_DTYPE)
        out_ref[0] = _dot(h, w2_ref[...])


def _compress(cache, layer, table, posc, w1bd, w2bd):
    nb, npg = table.shape
    rpp = cache.shape[2] // SLABS
    bpp = rpp // NSA_BLOCK
    nblk = npg * bpp
    n_chunks = NSA_BLOCK // R_CHUNK
    width = 2 * KV_WIDTH
    kern = functools.partial(_compress_kernel, layer=layer, npg=npg, bpp=bpp, n_steps=nb * n_chunks)
    return pl.pallas_call(
        kern,
        grid_spec=pltpu.PrefetchScalarGridSpec(
            num_scalar_prefetch=1,
            grid=(nb, n_chunks),
            in_specs=[pl.BlockSpec((NSA_BLOCK, width), lambda b, j, pt: (0, 0)),
                      pl.BlockSpec((R_CHUNK,) + w1bd.shape[1:], lambda b, j, pt: (j, 0, 0, 0)),
                      pl.BlockSpec(w2bd.shape, lambda b, j, pt: (0, 0)),
                      pl.BlockSpec(memory_space=pl.ANY)],
            out_specs=pl.BlockSpec((1, nblk, width), lambda b, j, pt: (b, 0, 0)),
            scratch_shapes=[pltpu.VMEM((2, nblk * BLK_SLOT, HEAD_DIM), F32),
                            pltpu.SemaphoreType.DMA((2,)),
                            pltpu.VMEM((nblk, width), F32)]),
        out_shape=jax.ShapeDtypeStruct((nb, nblk, width), F32),
        compiler_params=_params("arbitrary", "arbitrary"),
        name="nsa_compress",
    )(table.reshape(-1), posc, w1bd, w2bd, cache)


def _compress_weights(pos, w1, w2):
    posc = jnp.concatenate([pos[0]] * NSA_KV_HEADS + [pos[1]] * NSA_KV_HEADS, axis=1)
    w1r = w1.reshape(2, NSA_BLOCK, HEAD_DIM, HEAD_DIM)
    z = jnp.zeros_like(w1r)
    w1bd = jnp.concatenate([jnp.concatenate([w1r, z], axis=3), jnp.concatenate([z, w1r], axis=3)], axis=2)
    w1bd = jnp.transpose(w1bd, (1, 0, 2, 3)).astype(MXU_DTYPE)
    blocks = [w2[0]] * NSA_KV_HEADS + [w2[1]] * NSA_KV_HEADS
    n = len(blocks)
    w2bd = jnp.concatenate(
        [jnp.concatenate([blk if c == r else jnp.zeros_like(blk) for c in range(n)], axis=1)
         for r, blk in enumerate(blocks)], axis=0).astype(MXU_DTYPE)
    return posc, w1bd, w2bd


NSA_TQ = 256
NSA_KB = 256


def _fill_kv_scratch(kv_ref, head, n_heads, kb_ref, vt_ref, seq, chunk):
    slabs = 2 * n_heads
    for c in range(seq // chunk):
        rows = slice(c * chunk, (c + 1) * chunk)
        base = c * chunk * slabs
        kb_ref[rows, :] = kv_ref[pl.ds(base + head, chunk, stride=slabs), :].astype(MXU_DTYPE)
        vt_ref[:, rows] = kv_ref[pl.ds(base + n_heads + head, chunk, stride=slabs), :].T.astype(MXU_DTYPE)


def _online_softmax_step(xm, vt, m_ref, l_ref, acc_ref):
    m_old = m_ref[...]
    m_new = jnp.maximum(m_old, jnp.max(xm, axis=0, keepdims=True))
    alpha = jnp.exp(m_old - m_new)
    p = jnp.exp(xm - jnp.maximum(m_new, 0.1 * NEG_INF))
    l_ref[...] = alpha * l_ref[...] + jnp.sum(p, axis=0, keepdims=True)
    acc_ref[...] = alpha * acc_ref[...] + _dot(vt, p.astype(MXU_DTYPE))
    m_ref[...] = m_new


def _nsa_prompt_kernel(qraw_ref, qrot_ref, kc_ref, vc_ref, slc_ref, win_ref, ng_ref,
                       eblk_ref, out_ref, skb_ref, svt_ref, wkb_ref, wvt_ref, m_ref, l_ref, acc_ref,
                       *, seq, nb):
    head = pl.program_id(1)
    qi = pl.program_id(2)
    tq = NSA_TQ
    g4 = NSA_GROUP
    nq = g4 * tq
    t0 = qi * tq
    scale = HEAD_DIM ** -0.5
    n_sel = min(NSA_TOPN, nb)

    @pl.when(qi == 0)
    def _():
        _fill_kv_scratch(slc_ref, head, NSA_KV_HEADS, skb_ref, svt_ref, seq, NSA_KB)
        _fill_kv_scratch(win_ref, head, NSA_KV_HEADS, wkb_ref, wvt_ref, seq, NSA_KB)

    def stack(ref):
        return jnp.concatenate([ref[:, g * HEAD_DIM:(g + 1) * HEAD_DIM] for g in range(g4)], axis=0)

    def tile_lanes(a):
        return jnp.concatenate([a] * g4, axis=1)

    qraw = stack(qraw_ref)
    qrot = stack(qrot_ref)
    t_q = t0 + lax.broadcasted_iota(jnp.int32, (1, tq), 1)
    t_lane = tile_lanes(t_q)

    kc = kc_ref[...].astype(MXU_DTYPE)
    s_c = (_dot_nt(kc, qraw) * scale)[0:nb]
    blk = lax.broadcasted_iota(jnp.int32, (nb, 1), 0)
    vis_c = ((blk + 1) * NSA_BLOCK - 1) <= t_lane
    x_c = jnp.where(vis_c, s_c, NEG_INF)
    e_c = jnp.where(vis_c, jnp.exp(x_c - jnp.max(x_c, axis=0, keepdims=True)), 0.0)
    den = jnp.sum(e_c, axis=0, keepdims=True)
    p_c = e_c / jnp.where(den > 0.0, den, 1.0)
    vc_t = vc_ref[...].T[:, 0:nb].astype(MXU_DTYPE)
    o_c = _dot(vc_t, p_c.astype(MXU_DTYPE))

    imp = p_c[:, 0:tq]
    for g in range(1, g4):
        imp = imp + p_c[:, g * tq:(g + 1) * tq]
    cur = t_q // NSA_BLOCK
    n_idx = lax.broadcasted_iota(jnp.int32, (nb, tq), 0)
    rank = jnp.zeros((nb, tq), jnp.int32)
    for mm in range(nb):
        row = imp[mm:mm + 1, :]
        ahead = jnp.where(row > imp, 1, jnp.where(row == imp, jnp.where(mm < n_idx, 1, 0), 0))
        rank = rank + jnp.where(mm < cur, ahead, 0)
    sel = jnp.where(n_idx == cur, 1.0, jnp.where(n_idx < cur, jnp.where(rank < n_sel - 1, 1.0, 0.0), 0.0))
    sel = jnp.concatenate([sel, jnp.zeros((LANES - nb, tq), F32)], axis=0).astype(MXU_DTYPE)

    m_ref[...] = jnp.full_like(m_ref, NEG_INF)
    l_ref[...] = jnp.zeros_like(l_ref)
    acc_ref[...] = jnp.zeros_like(acc_ref)
    n_chunks = (t0 + tq - 1) // NSA_KB + 1

    def slc_body(kj, carry):
        k0 = pl.multiple_of(kj * NSA_KB, NSA_KB)
        s = _dot_nt(skb_ref[pl.ds(k0, NSA_KB), :], qrot) * scale
        selm = tile_lanes(_dot(eblk_ref[kj], sel))
        kpos = k0 + lax.broadcasted_iota(jnp.int32, (NSA_KB, 1), 0)
        xm = jnp.where(selm > 0.5, jnp.where(kpos <= t_lane, s, NEG_INF), NEG_INF)
        _online_softmax_step(xm, svt_ref[:, pl.ds(k0, NSA_KB)], m_ref, l_ref, acc_ref)
        return carry

    lax.fori_loop(0, n_chunks, slc_body, 0)
    o_s = acc_ref[...] / l_ref[...]

    band = min(WINDOW + tq, seq)
    w0 = pl.multiple_of(jnp.clip(t0 + tq - band, 0, seq - band), tq)
    s_w = _dot_nt(wkb_ref[pl.ds(w0, band), :], qrot) * scale
    dpos = t_lane - (w0 + lax.broadcasted_iota(jnp.int32, (band, 1), 0))
    x_w = jnp.where(dpos >= 0, jnp.where(dpos < WINDOW, s_w, NEG_INF), NEG_INF)
    e_w = jnp.exp(x_w - jnp.max(x_w, axis=0, keepdims=True))
    o_w = _dot(wvt_ref[:, pl.ds(w0, band)], e_w.astype(MXU_DTYPE)) / jnp.sum(e_w, axis=0, keepdims=True)

    gates = ng_ref[...].T
    for g in range(g4):
        r = slice(g * tq, (g + 1) * tq)
        o = (gates[3 * g:3 * g + 1] * o_c[:, r] + gates[3 * g + 1:3 * g + 2] * o_s[:, r]
             + gates[3 * g + 2:3 * g + 3] * o_w[:, r])
        out_ref[:, g * HEAD_DIM:(g + 1) * HEAD_DIM] = o.T


def _nsa_prompt(qraw, qrot, kcv, slc, win, ng, batch, seq):
    m = qraw.shape[0]
    nb = seq // NSA_BLOCK
    assert nb <= LANES and seq % NSA_KB == 0 and seq % NSA_TQ == 0
    nq = seq // NSA_TQ
    nkc = seq // NSA_KB
    kcv_p = jnp.pad(kcv.reshape(batch, nb, 2 * KV_WIDTH), ((0, 0), (0, LANES - nb), (0, 0)))
    kpos = np.arange(seq).reshape(nkc, NSA_KB, 1) // NSA_BLOCK
    eblk = jnp.asarray(kpos == np.arange(LANES).reshape(1, 1, LANES), dtype=MXU_DTYPE)
    gw = NSA_GROUP * HEAD_DIM
    lanes = NSA_GROUP * NSA_TQ
    q_spec = pl.BlockSpec((NSA_TQ, gw), lambda b, h, q: (b * nq + q, h))

    kv_spec = pl.BlockSpec((seq * SLABS, HEAD_DIM), lambda b, h, q: (b, 0))

    return pl.pallas_call(
        functools.partial(_nsa_prompt_kernel, seq=seq, nb=nb),
        grid=(batch, NSA_KV_HEADS, nq),
        in_specs=[q_spec, q_spec,
                  pl.BlockSpec((None, LANES, HEAD_DIM), lambda b, h, q: (b, 0, h)),
                  pl.BlockSpec((None, LANES, HEAD_DIM), lambda b, h, q: (b, 0, NSA_KV_HEADS + h)),
                  kv_spec, kv_spec,
                  pl.BlockSpec((NSA_TQ, LANES), lambda b, h, q: (b * nq + q, h)),
                  pl.BlockSpec(eblk.shape, lambda b, h, q: (0, 0, 0))],
        out_specs=pl.BlockSpec((NSA_TQ, gw), lambda b, h, q: (b * nq + q, h)),
        out_shape=jax.ShapeDtypeStruct((m, NSA_HEADS * HEAD_DIM), F32),
        scratch_shapes=[pltpu.VMEM((seq, HEAD_DIM), MXU_DTYPE), pltpu.VMEM((HEAD_DIM, seq), MXU_DTYPE),
                        pltpu.VMEM((seq, HEAD_DIM), MXU_DTYPE), pltpu.VMEM((HEAD_DIM, seq), MXU_DTYPE),
                        pltpu.VMEM((1, lanes), F32), pltpu.VMEM((1, lanes), F32),
                        pltpu.VMEM((HEAD_DIM, lanes), F32)],
        compiler_params=_params("arbitrary", "arbitrary", "arbitrary"),
        name="nsa_prompt",
    )(qraw, qrot, kcv_p, kcv_p, slc, win, ng, eblk)


SB_TQ = 512
SB_KB = 256


def _suffix_matrix_rows(kb):
    s = np.arange(kb + 8).reshape(kb + 8, 1)
    j = np.arange(kb).reshape(1, kb)
    return jnp.asarray(np.logical_or(j > s, s >= kb), dtype=MXU_DTYPE)


def _sb_prompt_kernel(q_ref, kv_ref, lw_ref, out_ref, kb_ref, vt_ref, tail_ref, acc_ref, *, seq):
    qi = pl.program_id(2)
    tq = SB_TQ
    t0 = qi * tq
    scale = HEAD_DIM ** -0.5
    kb = SB_KB

    @pl.when(qi == 0)
    def _():
        _fill_kv_scratch(kv_ref, pl.program_id(1), SB_HEADS, kb_ref, vt_ref, seq, kb)

    q = q_ref[...]
    t_lane = t0 + lax.broadcasted_iota(jnp.int32, (1, tq), 1)
    tail_ref[...] = jnp.zeros_like(tail_ref)
    acc_ref[...] = jnp.zeros_like(acc_ref)
    n_chunks = (t0 + tq) // kb

    def body(kk, carry):
        k0 = pl.multiple_of((n_chunks - 1 - kk) * kb, kb)
        z = _dot_nt(kb_ref[pl.ds(k0, kb), :], q) * scale
        valid = (k0 + lax.broadcasted_iota(jnp.int32, (kb, 1), 0)) < t_lane
        sp = _softplus(z)
        hi, lo = _split(jnp.where(valid, -sp, 0.0))
        sums = _dot(lw_ref[...], hi) + _dot(lw_ref[...], lo)
        tail = tail_ref[...]
        a = jnp.where(valid, jnp.exp(z - sp + sums[:kb] + tail), 0.0)
        acc_ref[...] += _dot(vt_ref[:, pl.ds(k0, kb)], a.astype(MXU_DTYPE))
        tail_ref[...] = tail + sums[kb:kb + 1]
        return carry

    lax.fori_loop(0, n_chunks, body, 0)
    out_ref[...] = acc_ref[...].T


def _sb_prompt(sbq, sbkv, batch, seq):
    m = sbq.shape[0]
    assert seq % SB_TQ == 0 and SB_TQ % SB_KB == 0
    nq = seq // SB_TQ
    lw = _suffix_matrix_rows(SB_KB)
    return pl.pallas_call(
        functools.partial(_sb_prompt_kernel, seq=seq),
        grid=(batch, SB_HEADS, nq),
        in_specs=[pl.BlockSpec((SB_TQ, HEAD_DIM), lambda b, h, q: (b * nq + q, h)),
                  pl.BlockSpec((seq * 2 * SB_HEADS, HEAD_DIM), lambda b, h, q: (b, 0)),
                  pl.BlockSpec(lw.shape, lambda b, h, q: (0, 0))],
        out_specs=pl.BlockSpec((SB_TQ, HEAD_DIM), lambda b, h, q: (b * nq + q, h)),
        out_shape=jax.ShapeDtypeStruct((m, SB_HEADS * HEAD_DIM), F32),
        scratch_shapes=[pltpu.VMEM((seq, HEAD_DIM), MXU_DTYPE), pltpu.VMEM((HEAD_DIM, seq), MXU_DTYPE),
                        pltpu.VMEM((1, SB_TQ), F32), pltpu.VMEM((HEAD_DIM, SB_TQ), F32)],
        compiler_params=_params("arbitrary", "arbitrary", "arbitrary"),
        name="sb_prompt",
    )(sbq, sbkv, lw)


DIFF_TQ = 512
DIFF_KB = 256


def _diff_lambda(lam_ref, lam_init):
    lv = lam_ref[...]
    s1 = jnp.sum(lv[0:1] * lv[1:2], axis=1, keepdims=True)
    s2 = jnp.sum(lv[2:3] * lv[3:4], axis=1, keepdims=True)
    return jnp.exp(s1) - jnp.exp(s2) + lam_init


def _diff_finish(o0, o1, lam, g, lam_init):
    o = o0 - lam * o1
    o = o * lax.rsqrt(jnp.mean(o * o, axis=-1, keepdims=True) + LN_EPS)
    return o * g * (1.0 - lam_init)


def _diff_prompt_kernel(q_ref, kv_ref, lam_ref, g_ref, out_ref, kb_ref, vt_ref, m_ref, l_ref, acc_ref,
                        *, lam_init, seq):
    qi = pl.program_id(2)
    tq = DIFF_TQ
    t0 = qi * tq
    scale = DIFF_DIM ** -0.5
    kb = DIFF_KB

    @pl.when(qi == 0)
    def _():
        _fill_kv_scratch(kv_ref, pl.program_id(1), DIFF_HEADS, kb_ref, vt_ref, seq, kb)

    q = q_ref[...]
    lane = lax.broadcasted_iota(jnp.int32, (tq, HEAD_DIM), 1)
    zero = jnp.zeros_like(q)
    qq = jnp.concatenate([jnp.where(lane < DIFF_DIM, q, zero), jnp.where(lane >= DIFF_DIM, q, zero)], axis=0)
    t_q = t0 + lax.broadcasted_iota(jnp.int32, (1, tq), 1)
    t_lane = jnp.concatenate([t_q, t_q], axis=1)
    m_ref[...] = jnp.full_like(m_ref, NEG_INF)
    l_ref[...] = jnp.zeros_like(l_ref)
    acc_ref[...] = jnp.zeros_like(acc_ref)

    def body(kj, carry):
        k0 = pl.multiple_of(kj * kb, kb)
        s = _dot_nt(kb_ref[pl.ds(k0, kb), :], qq) * scale
        kpos = k0 + lax.broadcasted_iota(jnp.int32, (kb, 1), 0)
        _online_softmax_step(jnp.where(kpos <= t_lane, s, NEG_INF), vt_ref[:, pl.ds(k0, kb)],
                             m_ref, l_ref, acc_ref)
        return carry

    lax.fori_loop(0, (t0 + tq) // kb, body, 0)
    o = acc_ref[...] / l_ref[...]
    lam = _diff_lambda(lam_ref, lam_init)
    out_ref[...] = _diff_finish(o[:, :tq].T, o[:, tq:].T, lam, g_ref[...], lam_init)


def _diff_prompt(dq, dkv, lam_rows, norm_g, lam_init, batch, seq):
    m = dq.shape[0]
    assert seq % DIFF_TQ == 0 and DIFF_TQ % DIFF_KB == 0
    nq = seq // DIFF_TQ
    return pl.pallas_call(
        functools.partial(_diff_prompt_kernel, lam_init=lam_init, seq=seq),
        grid=(batch, DIFF_HEADS, nq),
        in_specs=[pl.BlockSpec((DIFF_TQ, HEAD_DIM), lambda b, h, q: (b * nq + q, h)),
                  pl.BlockSpec((seq * 2 * DIFF_HEADS, HEAD_DIM), lambda b, h, q: (b, 0)),
                  pl.BlockSpec(lam_rows.shape, lambda b, h, q: (0, 0)),
                  pl.BlockSpec((1, HEAD_DIM), lambda b, h, q: (0, 0))],
        out_specs=pl.BlockSpec((DIFF_TQ, HEAD_DIM), lambda b, h, q: (b * nq + q, h)),
        out_shape=jax.ShapeDtypeStruct((m, DIFF_HEADS * HEAD_DIM), F32),
        scratch_shapes=[pltpu.VMEM((seq, HEAD_DIM), MXU_DTYPE), pltpu.VMEM((HEAD_DIM, seq), MXU_DTYPE),
                        pltpu.VMEM((1, 2 * DIFF_TQ), F32), pltpu.VMEM((1, 2 * DIFF_TQ), F32),
                        pltpu.VMEM((HEAD_DIM, 2 * DIFF_TQ), F32)],
        compiler_params=_params("arbitrary", "arbitrary", "arbitrary"),
        name="diff_prompt",
    )(dq, dkv, lam_rows, norm_g.reshape(1, HEAD_DIM))


def _merge_kernel(on_ref, os_ref, od_ref, mg_ref, x_ref, wo_ref, g_ref, b_ref, y_ref, yb_ref, *, alpha):
    n1 = on_ref.shape[1]
    n2 = n1 + os_ref.shape[1]
    r = _dot((on_ref[...] * mg_ref[:, :n1]).astype(MXU_DTYPE), wo_ref[:n1, :])
    r = r + _dot((os_ref[...] * mg_ref[:, n1:n2]).astype(MXU_DTYPE), wo_ref[n1:n2, :])
    r = r + _dot((od_ref[...] * mg_ref[:, n2:]).astype(MXU_DTYPE), wo_ref[n2:, :])
    y = _layer_norm(alpha * x_ref[...] + r, g_ref[...], b_ref[...])
    y_ref[...] = y
    yb_ref[...] = y.astype(yb_ref.dtype)


def _merge(o_nsa, o_sb, o_diff, mg, x, wo, g, b, alpha, tm):
    m, d = x.shape

    def row(width):
        return pl.BlockSpec((tm, width), lambda i: (i, 0))

    def full(a):
        return pl.BlockSpec(a.shape, lambda i: (0, 0))

    g2, b2 = g.reshape(1, d), b.reshape(1, d)
    return pl.pallas_call(
        functools.partial(_merge_kernel, alpha=alpha),
        grid=(m // tm,),
        in_specs=[row(o_nsa.shape[1]), row(o_sb.shape[1]), row(o_diff.shape[1]), row(mg.shape[1]), row(d),
                  full(wo), full(g2), full(b2)],
        out_specs=[row(d), row(d)],
        out_shape=[jax.ShapeDtypeStruct((m, d), F32), jax.ShapeDtypeStruct((m, d), MXU_DTYPE)],
        compiler_params=_params("arbitrary"),
        name="merge_out_proj",
    )(o_nsa, o_sb, o_diff, mg, x, wo, g2, b2)


FFN_TF = 1024


def _ffn_kernel(x_ref, xb_ref, wu_ref, wd_ref, g_ref, b_ref, y_ref, yb_ref, *, alpha):
    j = pl.program_id(1)

    @pl.when(j == 0)
    def _():
        y_ref[...] = jnp.zeros_like(y_ref)

    h = jnp.maximum(_dot(xb_ref[...], wu_ref[...]), 0.0)
    y_ref[...] += _dot((h * h).astype(MXU_DTYPE), wd_ref[...])

    @pl.when(j == pl.num_programs(1) - 1)
    def _():
        y = _layer_norm(alpha * x_ref[...] + y_ref[...], g_ref[...], b_ref[...])
        y_ref[...] = y
        yb_ref[...] = y.astype(yb_ref.dtype)


def _ffn(x, xb, wu, wd, g, b, alpha, tm):
    m, d = x.shape
    dff = wu.shape[1]
    g2, b2 = g.reshape(1, d), b.reshape(1, d)
    return pl.pallas_call(
        functools.partial(_ffn_kernel, alpha=alpha),
        grid=(m // tm, dff // FFN_TF),
        in_specs=[pl.BlockSpec((tm, d), lambda i, j: (i, 0), pipeline_mode=pl.Buffered(1)),
                  pl.BlockSpec((tm, d), lambda i, j: (i, 0), pipeline_mode=pl.Buffered(1)),
                  pl.BlockSpec((d, FFN_TF), lambda i, j: (0, j)),
                  pl.BlockSpec((FFN_TF, d), lambda i, j: (j, 0)),
                  pl.BlockSpec((1, d), lambda i, j: (0, 0)),
                  pl.BlockSpec((1, d), lambda i, j: (0, 0))],
        out_specs=[pl.BlockSpec((tm, d), lambda i, j: (i, 0)), pl.BlockSpec((tm, d), lambda i, j: (i, 0))],
        out_shape=[jax.ShapeDtypeStruct((m, d), F32), jax.ShapeDtypeStruct((m, d), MXU_DTYPE)],
        compiler_params=_params("arbitrary", "arbitrary"),
        name="ffn",
    )(x, xb, wu, wd, g2, b2)


def _expand_matrix(n_groups, width):
    c = np.arange(LANES).reshape(LANES, 1)
    j = np.arange(n_groups * width).reshape(1, -1) // width
    return jnp.asarray(c == j, dtype=MXU_DTYPE)


def _block_diag_q(q, n_heads, width, scale):
    b = q.shape[0]
    head = jnp.arange(n_heads * width) // width
    onehot = (head[:, None] == jnp.arange(LANES)[None, :]).astype(F32)
    return ((q.astype(F32) * scale)[:, :, None] * onehot[None]).astype(MXU_DTYPE)


def _select_kernel(qsel_ref, kcv_ref, e2_ref, oc_ref, idx_ref, *, pos_q, n_rows):
    half = KV_WIDTH
    kcv = kcv_ref[0]
    kc = kcv[:, :half].astype(MXU_DTYPE)
    vc = kcv[:, half:]
    n_idx = lax.broadcasted_iota(jnp.int32, (n_rows, LANES), 0)
    vis = ((n_idx + 1) * NSA_BLOCK - 1) <= pos_q
    imp = jnp.zeros((n_rows, LANES), F32)
    for g in range(NSA_GROUP):
        s = _dot(kc, qsel_ref[0, g])
        xm = jnp.where(vis, s, NEG_INF)
        mx = jnp.max(xm, axis=0, keepdims=True)
        e = jnp.where(vis, jnp.exp(xm - mx), 0.0)
        den = jnp.sum(e, axis=0, keepdims=True)
        p = e / jnp.where(den > 0.0, den, 1.0)
        imp = imp + p
        o = jnp.sum(_dot2(p, e2_ref[...]) * vc, axis=0, keepdims=True)
        for h in range(NSA_KV_HEADS):
            c = (h * NSA_GROUP + g) * HEAD_DIM
            oc_ref[0, :, c:c + HEAD_DIM] = o[:, h * HEAD_DIM:(h + 1) * HEAD_DIM]
    cur = pos_q // NSA_BLOCK
    n_f = n_idx.astype(F32)
    work = jnp.where(n_idx < cur, imp, -1.0)
    for k in range(NSA_TOPN - 1):
        mx = jnp.max(work, axis=0, keepdims=True)
        ix = jnp.min(jnp.where(work == mx, n_f, float(n_rows)), axis=0, keepdims=True)
        idx_ref[0, k:k + 1, :] = ix.astype(jnp.int32)
        work = jnp.where(n_f == ix, -2.0, work)
    idx_ref[0, NSA_TOPN - 1:NSA_TOPN, :] = jnp.full((1, LANES), cur, jnp.int32)


def _select(qsel, kcv, pos_q):
    nb, n_rows, width = kcv.shape
    e2 = _expand_matrix(NSA_KV_HEADS, HEAD_DIM)
    return pl.pallas_call(
        functools.partial(_select_kernel, pos_q=pos_q, n_rows=n_rows),
        grid=(nb,),
        in_specs=[pl.BlockSpec((1,) + qsel.shape[1:], lambda b: (b, 0, 0, 0)),
                  pl.BlockSpec((1, n_rows, width), lambda b: (b, 0, 0)),
                  pl.BlockSpec(e2.shape, lambda b: (0, 0))],
        out_specs=[pl.BlockSpec((1, 1, NSA_HEADS * HEAD_DIM), lambda b: (b, 0, 0)),
                   pl.BlockSpec((1, NSA_TOPN, LANES), lambda b: (b, 0, 0))],
        out_shape=[jax.ShapeDtypeStruct((nb, 1, NSA_HEADS * HEAD_DIM), F32),
                   jax.ShapeDtypeStruct((nb, NSA_TOPN, LANES), jnp.int32)],
        compiler_params=_params("arbitrary"),
        name="nsa_select",
    )(qsel, kcv, e2)


def _softmax_rows_plus_one(s, vis, s_new):
    xm = s if vis is None else jnp.where(vis, s, NEG_INF)
    mx = jnp.maximum(jnp.max(xm, axis=0, keepdims=True), s_new)
    e = jnp.exp(xm - mx)
    if vis is not None:
        e = jnp.where(vis, e, 0.0)
    e_new = jnp.exp(s_new - mx)
    den = jnp.sum(e, axis=0, keepdims=True) + e_new
    return e / den, e_new / den


def _nsa_sample_kernel(idx_ref, pt_ref, q_ref, oc_ref, slc_new_ref, win_new_ref, ws_ref, ng_ref, e4_ref,
                       cache_ref, out_ref, kvbuf, sem, *, layer, npg, wb):
    b = pl.program_id(0)
    n_old = NSA_TOPN - 1
    blk_rows = NSA_BLOCK * SLABS
    blocks_per_page = cache_ref.shape[2] // blk_rows

    def copies(fn):
        for h in range(NSA_KV_HEADS):
            for k in range(n_old):
                n = idx_ref[(b * NSA_TOPN + k) * NSA_KV_HEADS + h]
                page = pt_ref[b * npg + n // blocks_per_page]
                r0 = pl.multiple_of((n % blocks_per_page) * blk_rows, blk_rows)
                src = cache_ref.at[layer, page, pl.ds(r0, blk_rows), :]
                fn(pltpu.make_async_copy(src, kvbuf.at[h, pl.ds(k * blk_rows, blk_rows), :], sem.at[0]))

    copies(lambda cp: cp.start())
    copies(lambda cp: cp.wait())

    gw = NSA_GROUP * HEAD_DIM
    e4 = e4_ref[...]
    slc_new = slc_new_ref[0]
    win_new = win_new_ref[0]
    gates = ng_ref[0]
    n_keys = n_old * NSA_BLOCK
    i_w = lax.broadcasted_iota(jnp.int32, (wb, LANES), 0)
    vis_w = (wb - i_w) < WINDOW

    def attend(keys, vals, vis, k_new, v_new, qg):
        s = _dot(keys.astype(MXU_DTYPE), qg)
        s_new = _dot(jnp.concatenate([k_new] * 8, axis=0).astype(MXU_DTYPE), qg)[0:1]
        p, p_new = _softmax_rows_plus_one(s, vis, s_new)
        vt = jnp.concatenate([vals] * NSA_GROUP, axis=1)
        o = jnp.sum(_dot2(p, e4) * vt, axis=0, keepdims=True)
        o_new = _dot2(jnp.concatenate([p_new] * 8, axis=0), e4)[0:1] * jnp.concatenate([v_new] * NSA_GROUP, axis=1)
        return o + o_new

    for h in range(NSA_KV_HEADS):
        qg = q_ref[0, h]
        ks, vs = h * HEAD_DIM, KV_WIDTH + h * HEAD_DIM
        o_s = attend(_slab_rows(kvbuf.at[h], 0, n_keys, SLABS, [h]),
                     _slab_rows(kvbuf.at[h], 0, n_keys, SLABS, [NSA_KV_HEADS + h]), None,
                     slc_new[:, ks:ks + HEAD_DIM], slc_new[:, vs:vs + HEAD_DIM], qg)
        o_w = attend(_slab_rows(ws_ref.at[0], 0, wb, SLABS, [h]),
                     _slab_rows(ws_ref.at[0], 0, wb, SLABS, [NSA_KV_HEADS + h]), vis_w,
                     win_new[:, ks:ks + HEAD_DIM], win_new[:, vs:vs + HEAD_DIM], qg)
        o_c = oc_ref[0, :, h * gw:(h + 1) * gw]
        for g in range(NSA_GROUP):
            c = h * LANES + 3 * g
            sl = slice(g * HEAD_DIM, (g + 1) * HEAD_DIM)
            out_ref[0, :, h * gw + g * HEAD_DIM:h * gw + (g + 1) * HEAD_DIM] = (
                gates[:, c:c + 1] * o_c[:, sl] + gates[:, c + 1:c + 2] * o_s[:, sl]
                + gates[:, c + 2:c + 3] * o_w[:, sl])


def _nsa_sample(idx, table, qslc, o_c, slc_new, win_new, win_state, ng, cache, layer):
    nb, npg = table.shape
    wb = win_state.shape[2] // SLABS
    width = 2 * KV_WIDTH
    e4 = _expand_matrix(NSA_GROUP, HEAD_DIM)
    ws = win_state.reshape(win_state.shape[0] * nb, wb * SLABS, HEAD_DIM)
    idx_flat = idx[:, :, :NSA_KV_HEADS].reshape(-1)

    def one(width_):
        return pl.BlockSpec((1, 1, width_), lambda b, ix, pt: (b, 0, 0))

    return pl.pallas_call(
        functools.partial(_nsa_sample_kernel, layer=layer, npg=npg, wb=wb),
        grid_spec=pltpu.PrefetchScalarGridSpec(
            num_scalar_prefetch=2,
            grid=(nb,),
            in_specs=[pl.BlockSpec((1,) + qslc.shape[1:], lambda b, ix, pt: (b, 0, 0, 0)),
                      one(NSA_HEADS * HEAD_DIM), one(width), one(width),
                      pl.BlockSpec((1, wb * SLABS, HEAD_DIM), lambda b, ix, pt: (layer * nb + b, 0, 0)),
                      one(COL_TILE),
                      pl.BlockSpec(e4.shape, lambda b, ix, pt: (0, 0)),
                      pl.BlockSpec(memory_space=pl.ANY)],
            out_specs=one(NSA_HEADS * HEAD_DIM),
            scratch_shapes=[pltpu.VMEM((NSA_KV_HEADS, (NSA_TOPN - 1) * NSA_BLOCK * SLABS, HEAD_DIM), F32),
                            pltpu.SemaphoreType.DMA((1,))]),
        out_shape=jax.ShapeDtypeStruct((nb, 1, NSA_HEADS * HEAD_DIM), F32),
        compiler_params=_params("arbitrary"),
        name="nsa_sample",
    )(idx_flat, table.reshape(-1), qslc, o_c, slc_new, win_new, ws, ng, e4, cache)


PAGES_PER_STEP = 16
SB_DEC_KB = 256


def _page_pipeline(pt_ref, cache_ref, buf, sem, layer, npg, n_chunks, reverse):
    b = pl.program_id(0)
    c = pl.program_id(1)
    step = b * n_chunks + c
    slot = step % 2
    rpp = cache_ref.shape[2]

    def copies(bb, cc, sl, start):
        first = (n_chunks - 1 - cc if reverse else cc) * PAGES_PER_STEP
        for i in range(PAGES_PER_STEP):
            page = pt_ref[bb * npg + first + i]
            cp = pltpu.make_async_copy(cache_ref.at[layer, page], buf.at[sl, pl.ds(i * rpp, rpp), :], sem.at[sl])
            if start:
                cp.start(priority=i % 2)
            else:
                cp.wait()

    @pl.when(step == 0)
    def _():
        copies(b, c, slot, True)

    @pl.when(step + 1 < pl.num_programs(0) * n_chunks)
    def _():
        nxt = step + 1
        copies(nxt // n_chunks, nxt % n_chunks, 1 - slot, True)

    copies(b, c, slot, False)
    return slot


def _sb_sample_kernel(pt_ref, q_ref, lw_ref, e_ref, cache_ref, out_ref, buf, sem, tail_ref, acc_ref,
                      *, layer, npg, n_chunks):
    c = pl.program_id(1)
    slot = _page_pipeline(pt_ref, cache_ref, buf, sem, layer, npg, n_chunks, reverse=True)
    kw = SB_HEADS * HEAD_DIM
    kb = SB_DEC_KB

    @pl.when(c == 0)
    def _():
        tail_ref[...] = jnp.zeros_like(tail_ref)
        acc_ref[...] = jnp.zeros_like(acc_ref)

    qbd = q_ref[0]
    slabs = 2 * SB_HEADS
    n_keys = buf.shape[1] // slabs
    for j in reversed(range(n_keys // kb)):
        keys = _slab_rows(buf.at[slot], j * kb * slabs, kb, slabs, range(SB_HEADS))
        vals = _slab_rows(buf.at[slot], j * kb * slabs, kb, slabs, range(SB_HEADS, slabs))
        z = _dot(keys.astype(MXU_DTYPE), qbd)
        sp = _softplus(z)
        hi, lo = _split(-sp)
        suffix = _dot(lw_ref[...], hi) + _dot(lw_ref[...], lo)
        tail = tail_ref[0:1, :]
        a = jnp.exp(z - sp + suffix[:kb] + tail)
        contrib = _dot(a.astype(MXU_DTYPE), e_ref[...]) * vals
        acc_ref[...] += jnp.sum(contrib.reshape(kb // 8, 8, kw), axis=0)
        tail_ref[...] = tail_ref[...] + suffix[kb:kb + 8]

    @pl.when(c == n_chunks - 1)
    def _():
        out_ref[0] = jnp.sum(acc_ref[...], axis=0, keepdims=True)


def _sb_sample(table, qbd, cache, layer):
    nb, npg = table.shape
    flat_rows = cache.shape[2]
    rpp = flat_rows // (2 * SB_HEADS)
    assert npg % PAGES_PER_STEP == 0 and (PAGES_PER_STEP * rpp) % SB_DEC_KB == 0
    n_chunks = npg // PAGES_PER_STEP
    lw = _suffix_matrix_rows(SB_DEC_KB)
    e = _expand_matrix(SB_HEADS, HEAD_DIM)
    kw = SB_HEADS * HEAD_DIM
    return pl.pallas_call(
        functools.partial(_sb_sample_kernel, layer=layer, npg=npg, n_chunks=n_chunks),
        grid_spec=pltpu.PrefetchScalarGridSpec(
            num_scalar_prefetch=1,
            grid=(nb, n_chunks),
            in_specs=[pl.BlockSpec((1,) + qbd.shape[1:], lambda b, c, pt: (b, 0, 0)),
                      pl.BlockSpec(lw.shape, lambda b, c, pt: (0, 0)),
                      pl.BlockSpec(e.shape, lambda b, c, pt: (0, 0)),
                      pl.BlockSpec(memory_space=pl.ANY)],
            out_specs=pl.BlockSpec((1, 1, kw), lambda b, c, pt: (b, 0, 0)),
            scratch_shapes=[pltpu.VMEM((2, PAGES_PER_STEP * flat_rows, HEAD_DIM), F32),
                            pltpu.SemaphoreType.DMA((2,)),
                            pltpu.VMEM((8, LANES), F32),
                            pltpu.VMEM((8, kw), F32)]),
        out_shape=jax.ShapeDtypeStruct((nb, 1, kw), F32),
        compiler_params=_params("arbitrary", "arbitrary"),
        name="sb_sample",
    )(table.reshape(-1), qbd, lw, e, cache)


DIFF_DEC_KB = 512


def _diff_sample_kernel(pt_ref, q_ref, e0_ref, e1_ref, new_ref, lam_ref, g_ref, cache_ref, out_ref,
                        buf, sem, m_ref, l_ref, acc0_ref, acc1_ref, *, layer, npg, n_chunks, lam_init):
    c = pl.program_id(1)
    slot = _page_pipeline(pt_ref, cache_ref, buf, sem, layer, npg, n_chunks, reverse=False)
    kw = DIFF_HEADS * HEAD_DIM
    kb = DIFF_DEC_KB

    @pl.when(c == 0)
    def _():
        m_ref[...] = jnp.full_like(m_ref, NEG_INF)
        l_ref[...] = jnp.zeros_like(l_ref)
        acc0_ref[...] = jnp.zeros_like(acc0_ref)
        acc1_ref[...] = jnp.zeros_like(acc1_ref)

    qbd = q_ref[0]

    def absorb(keys, v, vis):
        n = keys.shape[0]
        s = _dot(keys.astype(MXU_DTYPE), qbd)
        if vis is not None:
            s = jnp.where(vis, s, NEG_INF)
        m_old = m_ref[...]
        m_new = jnp.maximum(m_old, jnp.max(s, axis=0, keepdims=True))
        alpha = jnp.exp(m_old - m_new)
        p = jnp.exp(s - m_new)
        if vis is not None:
            p = jnp.where(vis, p, 0.0)
        l_ref[...] = alpha * l_ref[...] + jnp.sum(p, axis=0, keepdims=True)
        m_ref[...] = m_new
        pb = p.astype(MXU_DTYPE)
        a8 = jnp.concatenate([alpha] * 8, axis=0)
        for e_ref, acc_ref in ((e0_ref, acc0_ref), (e1_ref, acc1_ref)):
            pe = _dot(pb, e_ref[...])
            part = jnp.sum((pe * v).reshape(n // 8, 8, kw), axis=0)
            acc_ref[...] = acc_ref[...] * _dot2(a8, e_ref[...]) + part

    slabs = 2 * DIFF_HEADS
    n_keys = buf.shape[1] // slabs
    for j in range(n_keys // kb):
        absorb(_slab_rows(buf.at[slot], j * kb * slabs, kb, slabs, range(DIFF_HEADS)),
               _slab_rows(buf.at[slot], j * kb * slabs, kb, slabs, range(DIFF_HEADS, slabs)), None)

    @pl.when(c == n_chunks - 1)
    def _():
        new8 = jnp.concatenate([new_ref[0]] * 8, axis=0)
        first = lax.broadcasted_iota(jnp.int32, (8, LANES), 0) == 0
        absorb(new8[:, :kw], new8[:, kw:], first)
        l8 = jnp.concatenate([l_ref[...]] * 8, axis=0)
        o0 = jnp.sum(acc0_ref[...], axis=0, keepdims=True) / _dot2(l8, e0_ref[...])[0:1]
        o1 = jnp.sum(acc1_ref[...], axis=0, keepdims=True) / _dot2(l8, e1_ref[...])[0:1]
        lam = _diff_lambda(lam_ref, lam_init)
        for h in range(DIFF_HEADS):
            sl = slice(h * HEAD_DIM, (h + 1) * HEAD_DIM)
            out_ref[0, :, sl] = _diff_finish(o0[:, sl], o1[:, sl], lam, g_ref[...], lam_init)


def _diff_sample(table, qbd, new_rows, lam_rows, norm_g, cache, layer, lam_init):
    nb, npg = table.shape
    flat_rows = cache.shape[2]
    rpp = flat_rows // (2 * DIFF_HEADS)
    width = new_rows.shape[2]
    assert npg % PAGES_PER_STEP == 0 and (PAGES_PER_STEP * rpp) % DIFF_DEC_KB == 0
    n_chunks = npg // PAGES_PER_STEP
    kw = DIFF_HEADS * HEAD_DIM
    both = np.arange(LANES).reshape(LANES, 1)
    col_head = np.arange(kw).reshape(1, kw) // HEAD_DIM
    e0 = jnp.asarray(both == 2 * col_head, dtype=MXU_DTYPE)
    e1 = jnp.asarray(both == 2 * col_head + 1, dtype=MXU_DTYPE)
    return pl.pallas_call(
        functools.partial(_diff_sample_kernel, layer=layer, npg=npg, n_chunks=n_chunks, lam_init=lam_init),
        grid_spec=pltpu.PrefetchScalarGridSpec(
            num_scalar_prefetch=1,
            grid=(nb, n_chunks),
            in_specs=[pl.BlockSpec((1,) + qbd.shape[1:], lambda b, c, pt: (b, 0, 0)),
                      pl.BlockSpec(e0.shape, lambda b, c, pt: (0, 0)),
                      pl.BlockSpec(e1.shape, lambda b, c, pt: (0, 0)),
                      pl.BlockSpec((1, 1, width), lambda b, c, pt: (b, 0, 0)),
                      pl.BlockSpec(lam_rows.shape, lambda b, c, pt: (0, 0)),
                      pl.BlockSpec((1, HEAD_DIM), lambda b, c, pt: (0, 0)),
                      pl.BlockSpec(memory_space=pl.ANY)],
            out_specs=pl.BlockSpec((1, 1, kw), lambda b, c, pt: (b, 0, 0)),
            scratch_shapes=[pltpu.VMEM((2, PAGES_PER_STEP * flat_rows, HEAD_DIM), F32),
                            pltpu.SemaphoreType.DMA((2,)),
                            pltpu.VMEM((1, LANES), F32), pltpu.VMEM((1, LANES), F32),
                            pltpu.VMEM((8, kw), F32), pltpu.VMEM((8, kw), F32)]),
        out_shape=jax.ShapeDtypeStruct((nb, 1, kw), F32),
        compiler_params=_params("arbitrary", "arbitrary"),
        name="diff_sample",
    )(table.reshape(-1), qbd, e0, e1, new_rows, lam_rows, norm_g.reshape(1, HEAD_DIM), cache)


SAMPLE_ROWS = 16


def _pick_tm(m, cap):
    tm = min(m, cap)
    while m % tm:
        tm //= 2
    return tm


def kernel(x_prompt, x_sample, cache_nsa_cmp_kv, cache_nsa_slc_kv, state_nsa_win_kv, cache_sb_kv,
           cache_diff_kv, page_table, w_in, w_o, ln1_g, ln1_b, ln2_g, ln2_b, w_up, w_down,
           nsa_cmp_pos, nsa_cmp_w1, nsa_cmp_w2, diff_lambda, diff_norm_g):
    depth = w_in.shape[0]
    batch, seq, d_model = x_prompt.shape
    nbs, dec_seq, _ = x_sample.shape
    assert dec_seq == 1 and nbs <= SAMPLE_ROWS
    npg = page_table.shape[1]
    n_pool, page = cache_nsa_cmp_kv.shape[1], cache_nsa_cmp_kv.shape[2]
    past = npg * page
    alpha = (2 * depth) ** 0.25
    width = 2 * KV_WIDTH
    mp = batch * seq
    tm_p = _pick_tm(seq, 1024)

    tabs_p = (_rope_tables(jnp.arange(seq), HEAD_DIM, HEAD_DIM // ROT_FRACTION)
              + _rope_tables(jnp.arange(seq), DIFF_DIM, DIFF_DIM // ROT_FRACTION))
    pos_s = jnp.full((SAMPLE_ROWS,), past, jnp.int32)
    tabs_s = (_rope_tables(pos_s, HEAD_DIM, HEAD_DIM // ROT_FRACTION)
              + _rope_tables(pos_s, DIFF_DIM, DIFF_DIM // ROT_FRACTION))

    cmp_cache = cache_nsa_cmp_kv.reshape(depth, n_pool, page * SLABS, HEAD_DIM)
    slc_cache = cache_nsa_slc_kv.reshape(depth, n_pool, page * SLABS, HEAD_DIM)
    sb_cache = cache_sb_kv.reshape(depth, n_pool, page * 2 * SB_HEADS, HEAD_DIM)
    diff_cache = cache_diff_kv.reshape(depth, n_pool, page * 2 * DIFF_HEADS, HEAD_DIM)
    win_state = state_nsa_win_kv.reshape(depth, nbs, state_nsa_win_kv.shape[2] * SLABS, HEAD_DIM)
    prompt_pages = jnp.arange(mp // page, dtype=jnp.int32).reshape(1, mp // page)
    new_pages = jnp.arange(nbs, dtype=jnp.int32).reshape(1, nbs)

    xp = x_prompt.reshape(mp, d_model)
    xs = jnp.pad(x_sample.reshape(nbs, d_model), ((0, SAMPLE_ROWS - nbs), (0, 0)))
    xp_b, xs_b = xp.astype(MXU_DTYPE), xs.astype(MXU_DTYPE)
    st_p = [[] for _ in range(5)]
    st_s = [[] for _ in range(5)]

    for l in range(depth):
        lam_init = 0.8 - 0.6 * math.exp(-0.3 * l)
        w_p = _permute_w_in(w_in[l])
        wo_b, wu_b, wd_b = w_o[l].astype(MXU_DTYPE), w_up[l].astype(MXU_DTYPE), w_down[l].astype(MXU_DTYPE)
        posc, w1bd, w2bd = _compress_weights(nsa_cmp_pos[l], nsa_cmp_w1[l], nsa_cmp_w2[l])
        lam_rows = diff_lambda[l].astype(F32)

        (qraw, qrot, cmp, slc, win, sbq, sbkv, dq, dkv, mg, ng) = _project(xp_b, w_p, tabs_p, tm_p, seq)
        kcv = _compress(cmp.reshape(1, mp // page, page * SLABS, HEAD_DIM), 0, prompt_pages, posc, w1bd, w2bd)
        o_nsa = _nsa_prompt(qraw, qrot, kcv, slc, win, ng, batch, seq)
        o_sb = _sb_prompt(sbq, sbkv, batch, seq)
        o_diff = _diff_prompt(dq, dkv, lam_rows, diff_norm_g[l], lam_init, batch, seq)
        x1, x1b = _merge(o_nsa, o_sb, o_diff, mg, xp, wo_b, ln1_g[l], ln1_b[l], alpha, _pick_tm(mp, 256))
        xp, xp_b = _ffn(x1, x1b, wu_b, wd_b, ln2_g[l], ln2_b[l], alpha, _pick_tm(mp, 512))
        win_rows = min(WINDOW, seq)
        st_p[0].append(cmp.reshape(batch, seq, 2, NSA_KV_HEADS, HEAD_DIM))
        st_p[1].append(slc.reshape(batch, seq, 2, NSA_KV_HEADS, HEAD_DIM))
        st_p[2].append(win.reshape(batch, seq, 2, NSA_KV_HEADS, HEAD_DIM)[:, seq - win_rows:])
        st_p[3].append(sbkv.reshape(batch, seq, 2, SB_HEADS, HEAD_DIM))
        st_p[4].append(dkv.reshape(batch, seq, 2, DIFF_HEADS, HEAD_DIM))

        (qraw, qrot, cmp, slc, win, sbq, sbkv, dq, dkv, mg, ng) = _project(
            xs_b, w_p, tabs_s, SAMPLE_ROWS, SAMPLE_ROWS)
        cmp, slc, win, sbkv, dkv = (a.reshape(SAMPLE_ROWS, -1) for a in (cmp, slc, win, sbkv, dkv))
        kcv_past = _compress(cmp_cache, l, page_table, posc, w1bd, w2bd)
        new_blk = jnp.zeros((1, nbs, NSA_BLOCK, width), F32).at[0, :, 0, :].set(cmp[:nbs])
        kcv_new = _compress(new_blk.reshape(1, nbs, NSA_BLOCK * SLABS, HEAD_DIM), 0, new_pages, posc, w1bd, w2bd)
        n_past = kcv_past.shape[1]
        n_rows = -(-(n_past + 1) // 8) * 8
        kcv_all = jnp.concatenate([kcv_past, kcv_new.reshape(nbs, 1, width),
                                   jnp.zeros((nbs, n_rows - n_past - 1, width), F32)], axis=1)
        scale = HEAD_DIM ** -0.5
        q4 = qraw[:nbs].reshape(nbs, NSA_KV_HEADS, NSA_GROUP, HEAD_DIM).transpose(0, 2, 1, 3)
        qsel = jnp.stack([_block_diag_q(q4[:, g].reshape(nbs, KV_WIDTH), NSA_KV_HEADS, HEAD_DIM, scale)
                          for g in range(NSA_GROUP)], axis=1)
        o_c, idx = _select(qsel, kcv_all, past)
        qr = qrot[:nbs].reshape(nbs, NSA_KV_HEADS, NSA_GROUP, HEAD_DIM).astype(F32) * scale
        qslc = jnp.pad(qr.transpose(0, 1, 3, 2), ((0, 0), (0, 0), (0, 0), (0, LANES - NSA_GROUP))).astype(MXU_DTYPE)
        o_nsa = _nsa_sample(idx, page_table, qslc, o_c, slc[:nbs].reshape(nbs, 1, width),
                            win[:nbs].reshape(nbs, 1, width), win_state, ng[:nbs].reshape(nbs, 1, COL_TILE),
                            slc_cache, l)
        o_sb = _sb_sample(page_table, _block_diag_q(sbq[:nbs], SB_HEADS, HEAD_DIM, scale), sb_cache, l)
        o_diff = _diff_sample(page_table, _block_diag_q(dq[:nbs], 2 * DIFF_HEADS, DIFF_DIM, DIFF_DIM ** -0.5),
                              dkv[:nbs].reshape(nbs, 1, -1), lam_rows, diff_norm_g[l], diff_cache, l, lam_init)

        def pad_rows(a):
            return jnp.pad(a.reshape(nbs, -1), ((0, SAMPLE_ROWS - nbs), (0, 0)))

        x1, x1b = _merge(pad_rows(o_nsa), pad_rows(o_sb), pad_rows(o_diff), mg, xs, wo_b, ln1_g[l], ln1_b[l],
                         alpha, SAMPLE_ROWS)
        xs, xs_b = _ffn(x1, x1b, wu_b, wd_b, ln2_g[l], ln2_b[l], alpha, SAMPLE_ROWS)
        new_win = win[:nbs].reshape(nbs, 1, 2, NSA_KV_HEADS, HEAD_DIM)
        wb = state_nsa_win_kv.shape[2]
        keep = min(WINDOW, past + 1)
        st_s[0].append(cmp[:nbs].reshape(nbs, 1, 2, NSA_KV_HEADS, HEAD_DIM))
        st_s[1].append(slc[:nbs].reshape(nbs, 1, 2, NSA_KV_HEADS, HEAD_DIM))
        st_s[2].append(jnp.concatenate([state_nsa_win_kv[l], new_win], axis=1)[:, wb + 1 - keep:])
        st_s[3].append(sbkv[:nbs].reshape(nbs, 1, 2, SB_HEADS, HEAD_DIM))
        st_s[4].append(dkv[:nbs].reshape(nbs, 1, 2, DIFF_HEADS, HEAD_DIM))

    return (xp.reshape(batch, seq, d_model), xs[:nbs].reshape(nbs, 1, d_model),
            jnp.stack(st_p[0]), jnp.stack(st_s[0]),
            jnp.stack(st_p[1]), jnp.stack(st_s[1]),
            jnp.stack(st_p[2]), jnp.stack(st_s[2]),
            jnp.stack(st_p[3]), jnp.stack(st_s[3]),
            jnp.stack(st_p[4]), jnp.stack(st_s[4]))
```

```python
import functools
import math

import numpy as np
import jax
import jax.numpy as jnp
from jax import lax
from jax.experimental import pallas as pl
from jax.experimental.pallas import tpu as pltpu

HEAD_DIM = 128
NSA_HEADS = 8
NSA_KV_HEADS = 2
NSA_GROUP = NSA_HEADS // NSA_KV_HEADS
NSA_BLOCK = 64
NSA_TOPN = 16
WINDOW = 512
SB_HEADS = 4
DIFF_HEADS = 4
DIFF_DIM = HEAD_DIM // 2
ROPE_THETA = 500000.0
ROT_FRACTION = 4
LN_EPS = 1e-5
NEG_INF = -1e30

F32 = jnp.float32
MXU_DTYPE = jnp.bfloat16

LANES = 128
COL_TILE = 256
VMEM_LIMIT_BYTES = 56 * 2**20

KV_WIDTH = NSA_KV_HEADS * HEAD_DIM
Q_TILES = NSA_HEADS * HEAD_DIM // COL_TILE
T_CMP = Q_TILES
T_SLC = T_CMP + 2
T_WIN = T_SLC + 2
T_SBQ = T_WIN + 2
T_SBKV = T_SBQ + SB_HEADS * HEAD_DIM // COL_TILE
T_DQ = T_SBKV + 2 * SB_HEADS * HEAD_DIM // COL_TILE
T_DKV = T_DQ + DIFF_HEADS * HEAD_DIM // COL_TILE
T_MG = T_DKV + 2 * DIFF_HEADS * HEAD_DIM // COL_TILE
MIX_WIDTH = (NSA_HEADS + SB_HEADS + DIFF_HEADS) * HEAD_DIM
T_NG = T_MG + MIX_WIDTH // COL_TILE
N_TILES = T_NG + 1


def _dot(a, b):
    return jnp.dot(a, b, preferred_element_type=F32)


def _dot_nt(a, b):
    return lax.dot_general(a, b, (((1,), (1,)), ((), ())), preferred_element_type=F32)


def _split(x):
    hi = x.astype(MXU_DTYPE)
    lo = (x - hi.astype(F32)).astype(MXU_DTYPE)
    return hi, lo


def _dot2(x, w):
    hi, lo = _split(x)
    return _dot(hi, w) + _dot(lo, w)


def _params(*sem):
    return pltpu.CompilerParams(dimension_semantics=sem, vmem_limit_bytes=VMEM_LIMIT_BYTES)


def _softplus(z):
    return jnp.maximum(z, 0.0) + jnp.log(1.0 + jnp.exp(-jnp.abs(z)))


def _layer_norm(h, g, b):
    mu = jnp.mean(h, axis=-1, keepdims=True)
    d = h - mu
    var = jnp.mean(d * d, axis=-1, keepdims=True)
    return d * lax.rsqrt(var + LN_EPS) * g + b


def _rope_tables(pos, width, rot_dim):
    half = rot_dim // 2
    inv_freq = ROPE_THETA ** (-jnp.arange(half, dtype=F32) / half)
    ang = pos.astype(F32)[:, None] * inv_freq[None, :]
    cos, sin = jnp.cos(ang), jnp.sin(ang)
    n = pos.shape[0]
    rest = width - 2 * half
    c = jnp.concatenate([cos, cos, jnp.ones((n, rest), F32)], axis=1)
    s_up = jnp.concatenate([jnp.zeros((n, half), F32), sin, jnp.zeros((n, rest), F32)], axis=1)
    s_dn = jnp.concatenate([-sin, jnp.zeros((n, half + rest), F32)], axis=1)
    rep = LANES // width
    c, s_up, s_dn = (jnp.tile(a, (1, rep)) for a in (c, s_up, s_dn))
    return (jnp.stack([c, jnp.ones_like(c)]), jnp.stack([s_up, jnp.zeros_like(c)]),
            jnp.stack([s_dn, jnp.zeros_like(c)]))


def _rope(a, c_ref, su_ref, sd_ref, shift):
    def one(h):
        return (h * c_ref[...] + pltpu.roll(h, shift, 1) * su_ref[...]
                + pltpu.roll(h, LANES - shift, 1) * sd_ref[...])
    return jnp.concatenate([one(a[:, :LANES]), one(a[:, LANES:])], axis=1)


def _put_slabs(ref, tile, val):
    rows = val.shape[0]
    slabs = ref.shape[0] // rows
    for k in range(COL_TILE // LANES):
        ref[pl.ds(2 * tile + k, rows, stride=slabs), :] = val[:, k * LANES:(k + 1) * LANES]


def _proj_kernel(x_ref, w_ref, *refs, kind, shift):
    acc = _dot_nt(x_ref[...], w_ref[...])
    if kind == "q":
        c_ref, su_ref, sd_ref, raw_ref, rot_ref = refs
        raw_ref[...] = acc.astype(raw_ref.dtype)
        rot_ref[...] = _rope(acc, c_ref, su_ref, sd_ref, shift).astype(rot_ref.dtype)
    elif kind == "kv":
        _put_slabs(refs[0], pl.program_id(1), acc)
    elif kind == "kv_rope":
        c_ref, su_ref, sd_ref, out_ref = refs
        _put_slabs(out_ref, pl.program_id(1), _rope(acc, c_ref, su_ref, sd_ref, shift))
    elif kind == "cast":
        refs[0][...] = acc.astype(refs[0].dtype)
    elif kind == "cast_rope":
        c_ref, su_ref, sd_ref, out_ref = refs
        out_ref[...] = _rope(acc, c_ref, su_ref, sd_ref, shift).astype(out_ref.dtype)
    else:
        assert kind == "sigmoid"
        refs[0][...] = jax.nn.sigmoid(acc)


def _proj_group(kind, xb, w_t, layer, tile0, n_tiles, tm, rows_per_seq, tabs=None, n_rope=0, shift=0,
                dtype=F32, slabs=0):
    m, d = xb.shape
    seq_tiles = rows_per_seq // tm
    in_specs = [pl.BlockSpec((tm, d), lambda i, n: (i, 0)),
                pl.BlockSpec((None, COL_TILE, d), lambda i, n: (layer, tile0 + n, 0))]
    args = [xb, w_t]
    if tabs is not None:
        tab_spec = pl.BlockSpec((None, tm, LANES),
                                lambda i, n: (jnp.where(n < n_rope, 0, 1), i % seq_tiles, 0))
        in_specs += [tab_spec] * 3
        args += list(tabs)
    if slabs:
        shape = jax.ShapeDtypeStruct((m * slabs, HEAD_DIM), F32)
        spec = pl.BlockSpec((tm * slabs, HEAD_DIM), lambda i, n: (i, 0))
    else:
        shape = jax.ShapeDtypeStruct((m, n_tiles * COL_TILE), dtype)
        spec = pl.BlockSpec((tm, COL_TILE), lambda i, n: (i, n))
    n_out = 2 if kind == "q" else 1
    out = pl.pallas_call(
        functools.partial(_proj_kernel, kind=kind, shift=shift),
        grid=(m // tm, n_tiles),
        in_specs=in_specs,
        out_specs=[spec] * n_out,
        out_shape=[shape] * n_out,
        compiler_params=_params("arbitrary", "arbitrary"),
        name="proj_" + kind,
    )(*args)
    return out if n_out > 1 else out[0]


def _project(xb, w_t, layer, tabs, tm, rows_per_seq):
    head_tabs, diff_tabs = tabs[:3], tabs[3:]
    s_head = HEAD_DIM // ROT_FRACTION // 2
    s_diff = DIFF_DIM // ROT_FRACTION // 2

    def group(kind, tile0, n_tiles, **kw):
        return _proj_group(kind, xb, w_t, layer, tile0, n_tiles, tm, rows_per_seq, **kw)

    qraw, qrot = group("q", 0, Q_TILES, tabs=head_tabs, n_rope=Q_TILES, shift=s_head, dtype=MXU_DTYPE)
    cmp = group("kv", T_CMP, 2, slabs=SLABS)
    slc = group("kv_rope", T_SLC, 2, tabs=head_tabs, n_rope=1, shift=s_head, slabs=SLABS)
    win = group("kv_rope", T_WIN, 2, tabs=head_tabs, n_rope=1, shift=s_head, slabs=SLABS)
    sbq = group("cast", T_SBQ, T_SBKV - T_SBQ, dtype=MXU_DTYPE)
    sbkv = group("kv", T_SBKV, T_DQ - T_SBKV, slabs=2 * SB_HEADS)
    n_dq = T_DKV - T_DQ
    dq = group("cast_rope", T_DQ, n_dq, tabs=diff_tabs, n_rope=n_dq, shift=s_diff, dtype=MXU_DTYPE)
    n_dkv = T_MG - T_DKV
    dkv = group("kv_rope", T_DKV, n_dkv, tabs=diff_tabs, n_rope=n_dkv // 2, shift=s_diff,
                slabs=2 * DIFF_HEADS)
    mg = group("sigmoid", T_MG, T_NG - T_MG)
    ng = group("sigmoid", T_NG, 1)
    return qraw, qrot, cmp, slc, win, sbq, sbkv, dq, dkv, mg, ng


def _arrange_w_in(w):
    wt = jnp.swapaxes(w, 1, 2)
    depth, _, d = wt.shape
    g0 = Q_TILES * COL_TILE + 6 * KV_WIDTH
    ng = 3 * NSA_HEADS
    per = ng // NSA_KV_HEADS
    pad = jnp.zeros((depth, LANES - per, d), w.dtype)
    return jnp.concatenate([wt[:, :g0], wt[:, g0 + ng:], wt[:, g0:g0 + per], pad,
                            wt[:, g0 + per:g0 + ng], pad], axis=1).astype(MXU_DTYPE)


R_CHUNK = 16
SLABS = 2 * NSA_KV_HEADS
SUBLANES = 8
BLK_SLOT = R_CHUNK * SLABS + SUBLANES


def _slab_rows(ref, first, count, stride, slabs):
    return jnp.concatenate([ref[pl.ds(first + c, count, stride=stride), :] for c in slabs], axis=1)


def _gelu(x):
    return 0.5 * x * (1.0 + jnp.tanh(math.sqrt(2.0 / math.pi) * (x + 0.044715 * (x * x * x))))


def _compress_kernel(pt_ref, pos_ref, w1_ref, w2_ref, cache_ref, out_ref, buf, sem, acc_ref,
                     *, layer, npg, bpp, n_steps):
    b = pl.program_id(0)
    j = pl.program_id(1)
    n_chunks = NSA_BLOCK // R_CHUNK
    step = b * n_chunks + j
    slot = step % 2
    nblk = npg * bpp

    def copy(bb, jj, sl, i, h):
        page = pt_ref[bb * npg + i]
        src = cache_ref.at[layer, page, pl.ds((h * NSA_BLOCK + jj * R_CHUNK) * SLABS, R_CHUNK * SLABS), :]
        dst = buf.at[sl, pl.ds((i * bpp + h) * BLK_SLOT, R_CHUNK * SLABS), :]
        return pltpu.make_async_copy(src, dst, sem.at[sl])

    def for_all(bb, jj, sl, start):
        def body(i, carry):
            for h in range(bpp):
                cp = copy(bb, jj, sl, i, h)
                if start:
                    cp.start(priority=h % 2)
                else:
                    cp.wait()
            return carry
        lax.fori_loop(0, npg, body, 0)

    @pl.when(step == 0)
    def _():
        for_all(b, j, slot, True)

    @pl.when(step + 1 < n_steps)
    def _():
        nxt = step + 1
        for_all(nxt // n_chunks, nxt % n_chunks, 1 - slot, True)

    for_all(b, j, slot, False)

    @pl.when(j == 0)
    def _():
        acc_ref[...] = jnp.zeros_like(acc_ref)

    half = 2 * HEAD_DIM
    a_k = jnp.zeros((nblk, half), F32)
    a_v = jnp.zeros((nblk, half), F32)
    for r in range(R_CHUNK):
        xr = _slab_rows(buf.at[slot], r * SLABS, nblk, BLK_SLOT, range(SLABS))
        xr = xr + pos_ref[pl.ds(j * R_CHUNK + r, 1), :]
        xb = xr.astype(MXU_DTYPE)
        a_k = a_k + _dot(xb[:, :half], w1_ref[r, 0])
        a_v = a_v + _dot(xb[:, half:], w1_ref[r, 1])
    acc_ref[:, :half] += a_k
    acc_ref[:, half:] += a_v

    @pl.when(j == n_chunks - 1)
    def _():
        h = _gelu(acc_ref[...]).astype(MXU_DTYPE)
        out_ref[0] = _dot(h, w2_ref[...])


def _compress(cache, layer, table, posc, w1bd, w2bd):
    nb, npg = table.shape
    rpp = cache.shape[2] // SLABS
    bpp = rpp // NSA_BLOCK
    nblk = npg * bpp
    n_chunks = NSA_BLOCK // R_CHUNK
    width = 2 * KV_WIDTH
    kern = functools.partial(_compress_kernel, layer=layer, npg=npg, bpp=bpp, n_steps=nb * n_chunks)
    return pl.pallas_call(
        kern,
        grid_spec=pltpu.PrefetchScalarGridSpec(
            num_scalar_prefetch=1,
            grid=(nb, n_chunks),
            in_specs=[pl.BlockSpec((NSA_BLOCK, width), lambda b, j, pt: (0, 0)),
                      pl.BlockSpec((R_CHUNK,) + w1bd.shape[1:], lambda b, j, pt: (j, 0, 0, 0)),
                      pl.BlockSpec(w2bd.shape, lambda b, j, pt: (0, 0)),
                      pl.BlockSpec(memory_space=pl.ANY)],
            out_specs=pl.BlockSpec((1, nblk, width), lambda b, j, pt: (b, 0, 0)),
            scratch_shapes=[pltpu.VMEM((2, nblk * BLK_SLOT, HEAD_DIM), F32),
                            pltpu.SemaphoreType.DMA((2,)),
                            pltpu.VMEM((nblk, width), F32)]),
        out_shape=jax.ShapeDtypeStruct((nb, nblk, width), F32),
        compiler_params=_params("arbitrary", "arbitrary"),
        name="nsa_compress",
    )(table.reshape(-1), posc, w1bd, w2bd, cache)


def _compress_weights(pos, w1, w2):
    posc = jnp.concatenate([pos[0]] * NSA_KV_HEADS + [pos[1]] * NSA_KV_HEADS, axis=1)
    w1r = w1.reshape(2, NSA_BLOCK, HEAD_DIM, HEAD_DIM)
    z = jnp.zeros_like(w1r)
    w1bd = jnp.concatenate([jnp.concatenate([w1r, z], axis=3), jnp.concatenate([z, w1r], axis=3)], axis=2)
    w1bd = jnp.transpose(w1bd, (1, 0, 2, 3)).astype(MXU_DTYPE)
    blocks = [w2[0]] * NSA_KV_HEADS + [w2[1]] * NSA_KV_HEADS
    n = len(blocks)
    w2bd = jnp.concatenate(
        [jnp.concatenate([blk if c == r else jnp.zeros_like(blk) for c in range(n)], axis=1)
         for r, blk in enumerate(blocks)], axis=0).astype(MXU_DTYPE)
    return posc, w1bd, w2bd


NSA_TQ = 256
NSA_KB = 256


def _fill_kv_scratch(kv_ref, head, n_heads, kb_ref, vt_ref, seq, chunk):
    slabs = 2 * n_heads
    for c in range(seq // chunk):
        rows = slice(c * chunk, (c + 1) * chunk)
        base = c * chunk * slabs
        kb_ref[rows, :] = kv_ref[pl.ds(base + head, chunk, stride=slabs), :].astype(MXU_DTYPE)
        vt_ref[:, rows] = kv_ref[pl.ds(base + n_heads + head, chunk, stride=slabs), :].T.astype(MXU_DTYPE)


def _online_softmax_step(xm, vt, m_ref, l_ref, acc_ref):
    m_old = m_ref[...]
    m_new = jnp.maximum(m_old, jnp.max(xm, axis=0, keepdims=True))
    alpha = jnp.exp(m_old - m_new)
    p = jnp.exp(xm - jnp.maximum(m_new, 0.1 * NEG_INF))
    l_ref[...] = alpha * l_ref[...] + jnp.sum(p, axis=0, keepdims=True)
    acc_ref[...] = alpha * acc_ref[...] + _dot(vt, p.astype(MXU_DTYPE))
    m_ref[...] = m_new


def _nsa_prompt_kernel(qraw_ref, qrot_ref, kc_ref, vc_ref, slc_ref, win_ref, ng_ref,
                       eblk_ref, out_ref, skb_ref, svt_ref, wkb_ref, wvt_ref, m_ref, l_ref, acc_ref,
                       *, seq, nb):
    head = pl.program_id(1)
    qi = pl.program_id(2)
    tq = NSA_TQ
    g4 = NSA_GROUP
    nq = g4 * tq
    t0 = qi * tq
    scale = HEAD_DIM ** -0.5
    n_sel = min(NSA_TOPN, nb)

    @pl.when(qi == 0)
    def _():
        _fill_kv_scratch(slc_ref, head, NSA_KV_HEADS, skb_ref, svt_ref, seq, NSA_KB)
        _fill_kv_scratch(win_ref, head, NSA_KV_HEADS, wkb_ref, wvt_ref, seq, NSA_KB)

    def stack(ref):
        return jnp.concatenate([ref[:, g * HEAD_DIM:(g + 1) * HEAD_DIM] for g in range(g4)], axis=0)

    def tile_lanes(a):
        return jnp.concatenate([a] * g4, axis=1)

    qraw = stack(qraw_ref)
    qrot = stack(qrot_ref)
    t_q = t0 + lax.broadcasted_iota(jnp.int32, (1, tq), 1)
    t_lane = tile_lanes(t_q)

    kc = kc_ref[...].astype(MXU_DTYPE)
    s_c = (_dot_nt(kc, qraw) * scale)[0:nb]
    blk = lax.broadcasted_iota(jnp.int32, (nb, 1), 0)
    vis_c = ((blk + 1) * NSA_BLOCK - 1) <= t_lane
    x_c = jnp.where(vis_c, s_c, NEG_INF)
    e_c = jnp.where(vis_c, jnp.exp(x_c - jnp.max(x_c, axis=0, keepdims=True)), 0.0)
    den = jnp.sum(e_c, axis=0, keepdims=True)
    p_c = e_c / jnp.where(den > 0.0, den, 1.0)
    vc_t = vc_ref[...].T[:, 0:nb].astype(MXU_DTYPE)
    o_c = _dot(vc_t, p_c.astype(MXU_DTYPE))

    imp = p_c[:, 0:tq]
    for g in range(1, g4):
        imp = imp + p_c[:, g * tq:(g + 1) * tq]
    cur = t_q // NSA_BLOCK
    n_idx = lax.broadcasted_iota(jnp.int32, (nb, tq), 0)
    rank = jnp.zeros((nb, tq), jnp.int32)
    for mm in range(nb):
        row = imp[mm:mm + 1, :]
        ahead = jnp.where(row > imp, 1, jnp.where(row == imp, jnp.where(mm < n_idx, 1, 0), 0))
        rank = rank + jnp.where(mm < cur, ahead, 0)
    sel = jnp.where(n_idx == cur, 1.0, jnp.where(n_idx < cur, jnp.where(rank < n_sel - 1, 1.0, 0.0), 0.0))
    sel = jnp.concatenate([sel, jnp.zeros((LANES - nb, tq), F32)], axis=0).astype(MXU_DTYPE)

    m_ref[...] = jnp.full_like(m_ref, NEG_INF)
    l_ref[...] = jnp.zeros_like(l_ref)
    acc_ref[...] = jnp.zeros_like(acc_ref)
    n_chunks = (t0 + tq - 1) // NSA_KB + 1

    def slc_body(kj, carry):
        k0 = pl.multiple_of(kj * NSA_KB, NSA_KB)
        s = _dot_nt(skb_ref[pl.ds(k0, NSA_KB), :], qrot) * scale
        selm = tile_lanes(_dot(eblk_ref[kj], sel))
        kpos = k0 + lax.broadcasted_iota(jnp.int32, (NSA_KB, 1), 0)
        xm = jnp.where(selm > 0.5, jnp.where(kpos <= t_lane, s, NEG_INF), NEG_INF)
        _online_softmax_step(xm, svt_ref[:, pl.ds(k0, NSA_KB)], m_ref, l_ref, acc_ref)
        return carry

    lax.fori_loop(0, n_chunks, slc_body, 0)
    o_s = acc_ref[...] / l_ref[...]

    band = min(WINDOW + tq, seq)
    w0 = pl.multiple_of(jnp.clip(t0 + tq - band, 0, seq - band), tq)
    s_w = _dot_nt(wkb_ref[pl.ds(w0, band), :], qrot) * scale
    dpos = t_lane - (w0 + lax.broadcasted_iota(jnp.int32, (band, 1), 0))
    x_w = jnp.where(dpos >= 0, jnp.where(dpos < WINDOW, s_w, NEG_INF), NEG_INF)
    e_w = jnp.exp(x_w - jnp.max(x_w, axis=0, keepdims=True))
    o_w = _dot(wvt_ref[:, pl.ds(w0, band)], e_w.astype(MXU_DTYPE)) / jnp.sum(e_w, axis=0, keepdims=True)

    gates = ng_ref[...].T
    for g in range(g4):
        r = slice(g * tq, (g + 1) * tq)
        o = (gates[3 * g:3 * g + 1] * o_c[:, r] + gates[3 * g + 1:3 * g + 2] * o_s[:, r]
             + gates[3 * g + 2:3 * g + 3] * o_w[:, r])
        out_ref[:, g * HEAD_DIM:(g + 1) * HEAD_DIM] = o.T


def _nsa_prompt(qraw, qrot, kcv, slc, win, ng, batch, seq):
    m = qraw.shape[0]
    nb = seq // NSA_BLOCK
    assert nb <= LANES and seq % NSA_KB == 0 and seq % NSA_TQ == 0
    nq = seq // NSA_TQ
    nkc = seq // NSA_KB
    kcv_p = jnp.pad(kcv.reshape(batch, nb, 2 * KV_WIDTH), ((0, 0), (0, LANES - nb), (0, 0)))
    kpos = np.arange(seq).reshape(nkc, NSA_KB, 1) // NSA_BLOCK
    eblk = jnp.asarray(kpos == np.arange(LANES).reshape(1, 1, LANES), dtype=MXU_DTYPE)
    gw = NSA_GROUP * HEAD_DIM
    lanes = NSA_GROUP * NSA_TQ
    q_spec = pl.BlockSpec((NSA_TQ, gw), lambda b, h, q: (b * nq + q, h))

    kv_spec = pl.BlockSpec((seq * SLABS, HEAD_DIM), lambda b, h, q: (b, 0))

    return pl.pallas_call(
        functools.partial(_nsa_prompt_kernel, seq=seq, nb=nb),
        grid=(batch, NSA_KV_HEADS, nq),
        in_specs=[q_spec, q_spec,
                  pl.BlockSpec((None, LANES, HEAD_DIM), lambda b, h, q: (b, 0, h)),
                  pl.BlockSpec((None, LANES, HEAD_DIM), lambda b, h, q: (b, 0, NSA_KV_HEADS + h)),
                  kv_spec, kv_spec,
                  pl.BlockSpec((NSA_TQ, LANES), lambda b, h, q: (b * nq + q, h)),
                  pl.BlockSpec(eblk.shape, lambda b, h, q: (0, 0, 0))],
        out_specs=pl.BlockSpec((NSA_TQ, gw), lambda b, h, q: (b * nq + q, h)),
        out_shape=jax.ShapeDtypeStruct((m, NSA_HEADS * HEAD_DIM), F32),
        scratch_shapes=[pltpu.VMEM((seq, HEAD_DIM), MXU_DTYPE), pltpu.VMEM((HEAD_DIM, seq), MXU_DTYPE),
                        pltpu.VMEM((seq, HEAD_DIM), MXU_DTYPE), pltpu.VMEM((HEAD_DIM, seq), MXU_DTYPE),
                        pltpu.VMEM((1, lanes), F32), pltpu.VMEM((1, lanes), F32),
                        pltpu.VMEM((HEAD_DIM, lanes), F32)],
        compiler_params=_params("arbitrary", "arbitrary", "arbitrary"),
        name="nsa_prompt",
    )(qraw, qrot, kcv_p, kcv_p, slc, win, ng, eblk)


SB_TQ = 512
SB_KB = 256


def _suffix_matrix_rows(kb):
    s = np.arange(kb + 8).reshape(kb + 8, 1)
    j = np.arange(kb).reshape(1, kb)
    return jnp.asarray(np.logical_or(j > s, s >= kb), dtype=MXU_DTYPE)


def _sb_prompt_kernel(q_ref, kv_ref, lw_ref, out_ref, kb_ref, vt_ref, tail_ref, acc_ref, *, seq):
    qi = pl.program_id(2)
    tq = SB_TQ
    t0 = qi * tq
    scale = HEAD_DIM ** -0.5
    kb = SB_KB

    @pl.when(qi == 0)
    def _():
        _fill_kv_scratch(kv_ref, pl.program_id(1), SB_HEADS, kb_ref, vt_ref, seq, kb)

    q = q_ref[...]
    t_lane = t0 + lax.broadcasted_iota(jnp.int32, (1, tq), 1)
    tail_ref[...] = jnp.zeros_like(tail_ref)
    acc_ref[...] = jnp.zeros_like(acc_ref)
    n_chunks = (t0 + tq) // kb

    def body(kk, carry):
        k0 = pl.multiple_of((n_chunks - 1 - kk) * kb, kb)
        z = _dot_nt(kb_ref[pl.ds(k0, kb), :], q) * scale
        valid = (k0 + lax.broadcasted_iota(jnp.int32, (kb, 1), 0)) < t_lane
        sp = _softplus(z)
        hi, lo = _split(jnp.where(valid, -sp, 0.0))
        sums = _dot(lw_ref[...], hi) + _dot(lw_ref[...], lo)
        tail = tail_ref[...]
        a = jnp.where(valid, jnp.exp(z - sp + sums[:kb] + tail), 0.0)
        acc_ref[...] += _dot(vt_ref[:, pl.ds(k0, kb)], a.astype(MXU_DTYPE))
        tail_ref[...] = tail + sums[kb:kb + 1]
        return carry

    lax.fori_loop(0, n_chunks, body, 0)
    out_ref[...] = acc_ref[...].T


def _sb_prompt(sbq, sbkv, batch, seq):
    m = sbq.shape[0]
    assert seq % SB_TQ == 0 and SB_TQ % SB_KB == 0
    nq = seq // SB_TQ
    lw = _suffix_matrix_rows(SB_KB)
    return pl.pallas_call(
        functools.partial(_sb_prompt_kernel, seq=seq),
        grid=(batch, SB_HEADS, nq),
        in_specs=[pl.BlockSpec((SB_TQ, HEAD_DIM), lambda b, h, q: (b * nq + q, h)),
                  pl.BlockSpec((seq * 2 * SB_HEADS, HEAD_DIM), lambda b, h, q: (b, 0)),
                  pl.BlockSpec(lw.shape, lambda b, h, q: (0, 0))],
        out_specs=pl.BlockSpec((SB_TQ, HEAD_DIM), lambda b, h, q: (b * nq + q, h)),
        out_shape=jax.ShapeDtypeStruct((m, SB_HEADS * HEAD_DIM), F32),
        scratch_shapes=[pltpu.VMEM((seq, HEAD_DIM), MXU_DTYPE), pltpu.VMEM((HEAD_DIM, seq), MXU_DTYPE),
                        pltpu.VMEM((1, SB_TQ), F32), pltpu.VMEM((HEAD_DIM, SB_TQ), F32)],
        compiler_params=_params("arbitrary", "arbitrary", "arbitrary"),
        name="sb_prompt",
    )(sbq, sbkv, lw)


DIFF_TQ = 512
DIFF_KB = 256


def _diff_lambda(lam_ref, lam_init):
    lv = lam_ref[...]
    s1 = jnp.sum(lv[0:1] * lv[1:2], axis=1, keepdims=True)
    s2 = jnp.sum(lv[2:3] * lv[3:4], axis=1, keepdims=True)
    return jnp.exp(s1) - jnp.exp(s2) + lam_init


def _diff_finish(o0, o1, lam, g, lam_init):
    o = o0 - lam * o1
    o = o * lax.rsqrt(jnp.mean(o * o, axis=-1, keepdims=True) + LN_EPS)
    return o * g * (1.0 - lam_init)


def _diff_prompt_kernel(q_ref, kv_ref, lam_ref, g_ref, out_ref, kb_ref, vt_ref, m_ref, l_ref, acc_ref,
                        *, lam_init, seq):
    qi = pl.program_id(2)
    tq = DIFF_TQ
    t0 = qi * tq
    scale = DIFF_DIM ** -0.5
    kb = DIFF_KB

    @pl.when(qi == 0)
    def _():
        _fill_kv_scratch(kv_ref, pl.program_id(1), DIFF_HEADS, kb_ref, vt_ref, seq, kb)

    q = q_ref[...]
    lane = lax.broadcasted_iota(jnp.int32, (tq, HEAD_DIM), 1)
    zero = jnp.zeros_like(q)
    qq = jnp.concatenate([jnp.where(lane < DIFF_DIM, q, zero), jnp.where(lane >= DIFF_DIM, q, zero)], axis=0)
    t_q = t0 + lax.broadcasted_iota(jnp.int32, (1, tq), 1)
    t_lane = jnp.concatenate([t_q, t_q], axis=1)
    m_ref[...] = jnp.full_like(m_ref, NEG_INF)
    l_ref[...] = jnp.zeros_like(l_ref)
    acc_ref[...] = jnp.zeros_like(acc_ref)

    def body(kj, carry):
        k0 = pl.multiple_of(kj * kb, kb)
        s = _dot_nt(kb_ref[pl.ds(k0, kb), :], qq) * scale
        kpos = k0 + lax.broadcasted_iota(jnp.int32, (kb, 1), 0)
        _online_softmax_step(jnp.where(kpos <= t_lane, s, NEG_INF), vt_ref[:, pl.ds(k0, kb)],
                             m_ref, l_ref, acc_ref)
        return carry

    lax.fori_loop(0, (t0 + tq) // kb, body, 0)
    o = acc_ref[...] / l_ref[...]
    lam = _diff_lambda(lam_ref, lam_init)
    out_ref[...] = _diff_finish(o[:, :tq].T, o[:, tq:].T, lam, g_ref[...], lam_init)


def _diff_prompt(dq, dkv, lam_rows, norm_g, lam_init, batch, seq):
    m = dq.shape[0]
    assert seq % DIFF_TQ == 0 and DIFF_TQ % DIFF_KB == 0
    nq = seq // DIFF_TQ
    return pl.pallas_call(
        functools.partial(_diff_prompt_kernel, lam_init=lam_init, seq=seq),
        grid=(batch, DIFF_HEADS, nq),
        in_specs=[pl.BlockSpec((DIFF_TQ, HEAD_DIM), lambda b, h, q: (b * nq + q, h)),
                  pl.BlockSpec((seq * 2 * DIFF_HEADS, HEAD_DIM), lambda b, h, q: (b, 0)),
                  pl.BlockSpec(lam_rows.shape, lambda b, h, q: (0, 0)),
                  pl.BlockSpec((1, HEAD_DIM), lambda b, h, q: (0, 0))],
        out_specs=pl.BlockSpec((DIFF_TQ, HEAD_DIM), lambda b, h, q: (b * nq + q, h)),
        out_shape=jax.ShapeDtypeStruct((m, DIFF_HEADS * HEAD_DIM), F32),
        scratch_shapes=[pltpu.VMEM((seq, HEAD_DIM), MXU_DTYPE), pltpu.VMEM((HEAD_DIM, seq), MXU_DTYPE),
                        pltpu.VMEM((1, 2 * DIFF_TQ), F32), pltpu.VMEM((1, 2 * DIFF_TQ), F32),
                        pltpu.VMEM((HEAD_DIM, 2 * DIFF_TQ), F32)],
        compiler_params=_params("arbitrary", "arbitrary", "arbitrary"),
        name="diff_prompt",
    )(dq, dkv, lam_rows, norm_g.reshape(1, HEAD_DIM))


def _merge_kernel(on_ref, os_ref, od_ref, mg_ref, x_ref, wo_ref, g_ref, b_ref, y_ref, yb_ref, *, alpha):
    n1 = on_ref.shape[1]
    n2 = n1 + os_ref.shape[1]
    r = _dot((on_ref[...] * mg_ref[:, :n1]).astype(MXU_DTYPE), wo_ref[:n1, :])
    r = r + _dot((os_ref[...] * mg_ref[:, n1:n2]).astype(MXU_DTYPE), wo_ref[n1:n2, :])
    r = r + _dot((od_ref[...] * mg_ref[:, n2:]).astype(MXU_DTYPE), wo_ref[n2:, :])
    y = _layer_norm(alpha * x_ref[...] + r, g_ref[...], b_ref[...])
    y_ref[...] = y
    yb_ref[...] = y.astype(yb_ref.dtype)


def _merge(o_nsa, o_sb, o_diff, mg, x, wo, layer, g, b, alpha, tm):
    m, d = x.shape

    def row(width):
        return pl.BlockSpec((tm, width), lambda i: (i, 0))

    def full(a):
        return pl.BlockSpec(a.shape, lambda i: (0, 0))

    g2, b2 = g.reshape(1, d), b.reshape(1, d)
    return pl.pallas_call(
        functools.partial(_merge_kernel, alpha=alpha),
        grid=(m // tm,),
        in_specs=[row(o_nsa.shape[1]), row(o_sb.shape[1]), row(o_diff.shape[1]), row(mg.shape[1]), row(d),
                  pl.BlockSpec((None,) + wo.shape[1:], lambda i: (layer, 0, 0)), full(g2), full(b2)],
        out_specs=[row(d), row(d)],
        out_shape=[jax.ShapeDtypeStruct((m, d), F32), jax.ShapeDtypeStruct((m, d), MXU_DTYPE)],
        compiler_params=_params("arbitrary"),
        name="merge_out_proj",
    )(o_nsa, o_sb, o_diff, mg, x, wo, g2, b2)


FFN_TF = 1024


def _ffn_kernel(x_ref, xb_ref, wu_ref, wd_ref, g_ref, b_ref, y_ref, yb_ref, acc_ref, *, alpha):
    j = pl.program_id(1)

    @pl.when(j == 0)
    def _():
        acc_ref[...] = jnp.zeros_like(acc_ref)

    h = jnp.maximum(_dot(xb_ref[...], wu_ref[...]), 0.0)
    acc_ref[...] += _dot((h * h).astype(MXU_DTYPE), wd_ref[...])

    @pl.when(j == pl.num_programs(1) - 1)
    def _():
        y = _layer_norm(alpha * x_ref[...] + acc_ref[...], g_ref[...], b_ref[...])
        y_ref[...] = y
        yb_ref[...] = y.astype(yb_ref.dtype)


def _ffn(x, xb, wu, wd, layer, g, b, alpha, tm):
    m, d = x.shape
    dff = wu.shape[2]
    g2, b2 = g.reshape(1, d), b.reshape(1, d)
    return pl.pallas_call(
        functools.partial(_ffn_kernel, alpha=alpha),
        grid=(m // tm, dff // FFN_TF),
        in_specs=[pl.BlockSpec((tm, d), lambda i, j: (i, 0)),
                  pl.BlockSpec((tm, d), lambda i, j: (i, 0)),
                  pl.BlockSpec((None, d, FFN_TF), lambda i, j: (layer, 0, j)),
                  pl.BlockSpec((None, FFN_TF, d), lambda i, j: (layer, j, 0)),
                  pl.BlockSpec((1, d), lambda i, j: (0, 0)),
                  pl.BlockSpec((1, d), lambda i, j: (0, 0))],
        out_specs=[pl.BlockSpec((tm, d), lambda i, j: (i, 0)), pl.BlockSpec((tm, d), lambda i, j: (i, 0))],
        out_shape=[jax.ShapeDtypeStruct((m, d), F32), jax.ShapeDtypeStruct((m, d), MXU_DTYPE)],
        scratch_shapes=[pltpu.VMEM((tm, d), F32)],
        compiler_params=_params("arbitrary", "arbitrary"),
        name="ffn",
    )(x, xb, wu, wd, g2, b2)


def _expand_matrix(n_groups, width):
    c = np.arange(LANES).reshape(LANES, 1)
    j = np.arange(n_groups * width).reshape(1, -1) // width
    return jnp.asarray(c == j, dtype=MXU_DTYPE)


def _block_diag_q(q, n_heads, width, scale):
    b = q.shape[0]
    head = jnp.arange(n_heads * width) // width
    onehot = (head[:, None] == jnp.arange(LANES)[None, :]).astype(F32)
    return ((q.astype(F32) * scale)[:, :, None] * onehot[None]).astype(MXU_DTYPE)


def _select_kernel(qsel_ref, kcv_ref, e2_ref, oc_ref, idx_ref, *, pos_q, n_rows):
    half = KV_WIDTH
    kcv = kcv_ref[0]
    kc = kcv[:, :half].astype(MXU_DTYPE)
    vc = kcv[:, half:]
    n_idx = lax.broadcasted_iota(jnp.int32, (n_rows, LANES), 0)
    vis = ((n_idx + 1) * NSA_BLOCK - 1) <= pos_q
    imp = jnp.zeros((n_rows, LANES), F32)
    for g in range(NSA_GROUP):
        s = _dot(kc, qsel_ref[0, g])
        xm = jnp.where(vis, s, NEG_INF)
        mx = jnp.max(xm, axis=0, keepdims=True)
        e = jnp.where(vis, jnp.exp(xm - mx), 0.0)
        den = jnp.sum(e, axis=0, keepdims=True)
        p = e / jnp.where(den > 0.0, den, 1.0)
        imp = imp + p
        o = jnp.sum(_dot2(p, e2_ref[...]) * vc, axis=0, keepdims=True)
        for h in range(NSA_KV_HEADS):
            c = (h * NSA_GROUP + g) * HEAD_DIM
            oc_ref[0, :, c:c + HEAD_DIM] = o[:, h * HEAD_DIM:(h + 1) * HEAD_DIM]
    cur = pos_q // NSA_BLOCK
    n_f = n_idx.astype(F32)
    work = jnp.where(n_idx < cur, imp, -1.0)
    for k in range(NSA_TOPN - 1):
        mx = jnp.max(work, axis=0, keepdims=True)
        ix = jnp.min(jnp.where(work == mx, n_f, float(n_rows)), axis=0, keepdims=True)
        idx_ref[0, k:k + 1, :] = ix.astype(jnp.int32)
        work = jnp.where(n_f == ix, -2.0, work)
    idx_ref[0, NSA_TOPN - 1:NSA_TOPN, :] = jnp.full((1, LANES), cur, jnp.int32)


def _select(qsel, kcv, pos_q):
    nb, n_rows, width = kcv.shape
    e2 = _expand_matrix(NSA_KV_HEADS, HEAD_DIM)
    return pl.pallas_call(
        functools.partial(_select_kernel, pos_q=pos_q, n_rows=n_rows),
        grid=(nb,),
        in_specs=[pl.BlockSpec((1,) + qsel.shape[1:], lambda b: (b, 0, 0, 0)),
                  pl.BlockSpec((1, n_rows, width), lambda b: (b, 0, 0)),
                  pl.BlockSpec(e2.shape, lambda b: (0, 0))],
        out_specs=[pl.BlockSpec((1, 1, NSA_HEADS * HEAD_DIM), lambda b: (b, 0, 0)),
                   pl.BlockSpec((1, NSA_TOPN, LANES), lambda b: (b, 0, 0))],
        out_shape=[jax.ShapeDtypeStruct((nb, 1, NSA_HEADS * HEAD_DIM), F32),
                   jax.ShapeDtypeStruct((nb, NSA_TOPN, LANES), jnp.int32)],
        compiler_params=_params("arbitrary"),
        name="nsa_select",
    )(qsel, kcv, e2)


def _softmax_rows_plus_one(s, vis, s_new):
    xm = s if vis is None else jnp.where(vis, s, NEG_INF)
    mx = jnp.maximum(jnp.max(xm, axis=0, keepdims=True), s_new)
    e = jnp.exp(xm - mx)
    if vis is not None:
        e = jnp.where(vis, e, 0.0)
    e_new = jnp.exp(s_new - mx)
    den = jnp.sum(e, axis=0, keepdims=True) + e_new
    return e / den, e_new / den


def _nsa_sample_kernel(idx_ref, pt_ref, q_ref, oc_ref, slc_new_ref, win_new_ref, ws_ref, ng_ref, e4_ref,
                       cache_ref, out_ref, kvbuf, sem, *, layer, npg, wb):
    b = pl.program_id(0)
    n_old = NSA_TOPN - 1
    blk_rows = NSA_BLOCK * SLABS
    blocks_per_page = cache_ref.shape[2] // blk_rows

    def copies(fn):
        for h in range(NSA_KV_HEADS):
            for k in range(n_old):
                n = idx_ref[(b * NSA_TOPN + k) * NSA_KV_HEADS + h]
                page = pt_ref[b * npg + n // blocks_per_page]
                r0 = pl.multiple_of((n % blocks_per_page) * blk_rows, blk_rows)
                src = cache_ref.at[layer, page, pl.ds(r0, blk_rows), :]
                fn(pltpu.make_async_copy(src, kvbuf.at[h, pl.ds(k * blk_rows, blk_rows), :], sem.at[0]))

    copies(lambda cp: cp.start())
    copies(lambda cp: cp.wait())

    gw = NSA_GROUP * HEAD_DIM
    e4 = e4_ref[...]
    slc_new = slc_new_ref[0]
    win_new = win_new_ref[0]
    gates = ng_ref[0]
    n_keys = n_old * NSA_BLOCK
    i_w = lax.broadcasted_iota(jnp.int32, (wb, LANES), 0)
    vis_w = (wb - i_w) < WINDOW

    def attend(keys, vals, vis, k_new, v_new, qg):
        s = _dot(keys.astype(MXU_DTYPE), qg)
        s_new = _dot(jnp.concatenate([k_new] * 8, axis=0).astype(MXU_DTYPE), qg)[0:1]
        p, p_new = _softmax_rows_plus_one(s, vis, s_new)
        vt = jnp.concatenate([vals] * NSA_GROUP, axis=1)
        o = jnp.sum(_dot2(p, e4) * vt, axis=0, keepdims=True)
        o_new = _dot2(jnp.concatenate([p_new] * 8, axis=0), e4)[0:1] * jnp.concatenate([v_new] * NSA_GROUP, axis=1)
        return o + o_new

    for h in range(NSA_KV_HEADS):
        qg = q_ref[0, h]
        ks, vs = h * HEAD_DIM, KV_WIDTH + h * HEAD_DIM
        o_s = attend(_slab_rows(kvbuf.at[h], 0, n_keys, SLABS, [h]),
                     _slab_rows(kvbuf.at[h], 0, n_keys, SLABS, [NSA_KV_HEADS + h]), None,
                     slc_new[:, ks:ks + HEAD_DIM], slc_new[:, vs:vs + HEAD_DIM], qg)
        o_w = attend(_slab_rows(ws_ref.at[0], 0, wb, SLABS, [h]),
                     _slab_rows(ws_ref.at[0], 0, wb, SLABS, [NSA_KV_HEADS + h]), vis_w,
                     win_new[:, ks:ks + HEAD_DIM], win_new[:, vs:vs + HEAD_DIM], qg)
        o_c = oc_ref[0, :, h * gw:(h + 1) * gw]
        for g in range(NSA_GROUP):
            c = h * LANES + 3 * g
            sl = slice(g * HEAD_DIM, (g + 1) * HEAD_DIM)
            out_ref[0, :, h * gw + g * HEAD_DIM:h * gw + (g + 1) * HEAD_DIM] = (
                gates[:, c:c + 1] * o_c[:, sl] + gates[:, c + 1:c + 2] * o_s[:, sl]
                + gates[:, c + 2:c + 3] * o_w[:, sl])


def _nsa_sample(idx, table, qslc, o_c, slc_new, win_new, win_state, ng, cache, layer):
    nb, npg = table.shape
    wb = win_state.shape[2] // SLABS
    width = 2 * KV_WIDTH
    e4 = _expand_matrix(NSA_GROUP, HEAD_DIM)
    ws = win_state.reshape(win_state.shape[0] * nb, wb * SLABS, HEAD_DIM)
    idx_flat = idx[:, :, :NSA_KV_HEADS].reshape(-1)

    def one(width_):
        return pl.BlockSpec((1, 1, width_), lambda b, ix, pt: (b, 0, 0))

    return pl.pallas_call(
        functools.partial(_nsa_sample_kernel, layer=layer, npg=npg, wb=wb),
        grid_spec=pltpu.PrefetchScalarGridSpec(
            num_scalar_prefetch=2,
            grid=(nb,),
            in_specs=[pl.BlockSpec((1,) + qslc.shape[1:], lambda b, ix, pt: (b, 0, 0, 0)),
                      one(NSA_HEADS * HEAD_DIM), one(width), one(width),
                      pl.BlockSpec((1, wb * SLABS, HEAD_DIM), lambda b, ix, pt: (layer * nb + b, 0, 0)),
                      one(COL_TILE),
                      pl.BlockSpec(e4.shape, lambda b, ix, pt: (0, 0)),
                      pl.BlockSpec(memory_space=pl.ANY)],
            out_specs=one(NSA_HEADS * HEAD_DIM),
            scratch_shapes=[pltpu.VMEM((NSA_KV_HEADS, (NSA_TOPN - 1) * NSA_BLOCK * SLABS, HEAD_DIM), F32),
                            pltpu.SemaphoreType.DMA((1,))]),
        out_shape=jax.ShapeDtypeStruct((nb, 1, NSA_HEADS * HEAD_DIM), F32),
        compiler_params=_params("arbitrary"),
        name="nsa_sample",
    )(idx_flat, table.reshape(-1), qslc, o_c, slc_new, win_new, ws, ng, e4, cache)


PAGES_PER_STEP = 16
SB_DEC_KB = 256


def _page_pipeline(pt_ref, cache_ref, buf, sem, layer, npg, n_chunks, reverse):
    b = pl.program_id(0)
    c = pl.program_id(1)
    step = b * n_chunks + c
    slot = step % 2
    rpp = cache_ref.shape[2]

    def copies(bb, cc, sl, start):
        first = (n_chunks - 1 - cc if reverse else cc) * PAGES_PER_STEP
        for i in range(PAGES_PER_STEP):
            page = pt_ref[bb * npg + first + i]
            cp = pltpu.make_async_copy(cache_ref.at[layer, page], buf.at[sl, pl.ds(i * rpp, rpp), :], sem.at[sl])
            if start:
                cp.start(priority=i % 2)
            else:
                cp.wait()

    @pl.when(step == 0)
    def _():
        copies(b, c, slot, True)

    @pl.when(step + 1 < pl.num_programs(0) * n_chunks)
    def _():
        nxt = step + 1
        copies(nxt // n_chunks, nxt % n_chunks, 1 - slot, True)

    copies(b, c, slot, False)
    return slot


def _sb_sample_kernel(pt_ref, q_ref, lw_ref, e_ref, cache_ref, out_ref, buf, sem, tail_ref, acc_ref,
                      *, layer, npg, n_chunks):
    c = pl.program_id(1)
    slot = _page_pipeline(pt_ref, cache_ref, buf, sem, layer, npg, n_chunks, reverse=True)
    kw = SB_HEADS * HEAD_DIM
    kb = SB_DEC_KB

    @pl.when(c == 0)
    def _():
        tail_ref[...] = jnp.zeros_like(tail_ref)
        acc_ref[...] = jnp.zeros_like(acc_ref)

    qbd = q_ref[0]
    slabs = 2 * SB_HEADS
    n_keys = buf.shape[1] // slabs
    for j in reversed(range(n_keys // kb)):
        keys = _slab_rows(buf.at[slot], j * kb * slabs, kb, slabs, range(SB_HEADS))
        vals = _slab_rows(buf.at[slot], j * kb * slabs, kb, slabs, range(SB_HEADS, slabs))
        z = _dot(keys.astype(MXU_DTYPE), qbd)
        sp = _softplus(z)
        hi, lo = _split(-sp)
        suffix = _dot(lw_ref[...], hi) + _dot(lw_ref[...], lo)
        tail = tail_ref[0:1, :]
        a = jnp.exp(z - sp + suffix[:kb] + tail)
        contrib = _dot(a.astype(MXU_DTYPE), e_ref[...]) * vals
        acc_ref[...] += jnp.sum(contrib.reshape(kb // 8, 8, kw), axis=0)
        tail_ref[...] = tail_ref[...] + suffix[kb:kb + 8]

    @pl.when(c == n_chunks - 1)
    def _():
        out_ref[0] = jnp.sum(acc_ref[...], axis=0, keepdims=True)


def _sb_sample(table, qbd, cache, layer):
    nb, npg = table.shape
    flat_rows = cache.shape[2]
    rpp = flat_rows // (2 * SB_HEADS)
    assert npg % PAGES_PER_STEP == 0 and (PAGES_PER_STEP * rpp) % SB_DEC_KB == 0
    n_chunks = npg // PAGES_PER_STEP
    lw = _suffix_matrix_rows(SB_DEC_KB)
    e = _expand_matrix(SB_HEADS, HEAD_DIM)
    kw = SB_HEADS * HEAD_DIM
    return pl.pallas_call(
        functools.partial(_sb_sample_kernel, layer=layer, npg=npg, n_chunks=n_chunks),
        grid_spec=pltpu.PrefetchScalarGridSpec(
            num_scalar_prefetch=1,
            grid=(nb, n_chunks),
            in_specs=[pl.BlockSpec((1,) + qbd.shape[1:], lambda b, c, pt: (b, 0, 0)),
                      pl.BlockSpec(lw.shape, lambda b, c, pt: (0, 0)),
                      pl.BlockSpec(e.shape, lambda b, c, pt: (0, 0)),
                      pl.BlockSpec(memory_space=pl.ANY)],
            out_specs=pl.BlockSpec((1, 1, kw), lambda b, c, pt: (b, 0, 0)),
            scratch_shapes=[pltpu.VMEM((2, PAGES_PER_STEP * flat_rows, HEAD_DIM), F32),
                            pltpu.SemaphoreType.DMA((2,)),
                            pltpu.VMEM((8, LANES), F32),
                            pltpu.VMEM((8, kw), F32)]),
        out_shape=jax.ShapeDtypeStruct((nb, 1, kw), F32),
        compiler_params=_params("arbitrary", "arbitrary"),
        name="sb_sample",
    )(table.reshape(-1), qbd, lw, e, cache)


DIFF_DEC_KB = 512


def _diff_sample_kernel(pt_ref, q_ref, e0_ref, e1_ref, new_ref, lam_ref, g_ref, cache_ref, out_ref,
                        buf, sem, m_ref, l_ref, acc0_ref, acc1_ref, *, layer, npg, n_chunks, lam_init):
    c = pl.program_id(1)
    slot = _page_pipeline(pt_ref, cache_ref, buf, sem, layer, npg, n_chunks, reverse=False)
    kw = DIFF_HEADS * HEAD_DIM
    kb = DIFF_DEC_KB

    @pl.when(c == 0)
    def _():
        m_ref[...] = jnp.full_like(m_ref, NEG_INF)
        l_ref[...] = jnp.zeros_like(l_ref)
        acc0_ref[...] = jnp.zeros_like(acc0_ref)
        acc1_ref[...] = jnp.zeros_like(acc1_ref)

    qbd = q_ref[0]

    def absorb(keys, v, vis):
        n = keys.shape[0]
        s = _dot(keys.astype(MXU_DTYPE), qbd)
        if vis is not None:
            s = jnp.where(vis, s, NEG_INF)
        m_old = m_ref[...]
        m_new = jnp.maximum(m_old, jnp.max(s, axis=0, keepdims=True))
        alpha = jnp.exp(m_old - m_new)
        p = jnp.exp(s - m_new)
        if vis is not None:
            p = jnp.where(vis, p, 0.0)
        l_ref[...] = alpha * l_ref[...] + jnp.sum(p, axis=0, keepdims=True)
        m_ref[...] = m_new
        pb = p.astype(MXU_DTYPE)
        a8 = jnp.concatenate([alpha] * 8, axis=0)
        for e_ref, acc_ref in ((e0_ref, acc0_ref), (e1_ref, acc1_ref)):
            pe = _dot(pb, e_ref[...])
            part = jnp.sum((pe * v).reshape(n // 8, 8, kw), axis=0)
            acc_ref[...] = acc_ref[...] * _dot2(a8, e_ref[...]) + part

    slabs = 2 * DIFF_HEADS
    n_keys = buf.shape[1] // slabs
    for j in range(n_keys // kb):
        absorb(_slab_rows(buf.at[slot], j * kb * slabs, kb, slabs, range(DIFF_HEADS)),
               _slab_rows(buf.at[slot], j * kb * slabs, kb, slabs, range(DIFF_HEADS, slabs)), None)

    @pl.when(c == n_chunks - 1)
    def _():
        new8 = jnp.concatenate([new_ref[0]] * 8, axis=0)
        first = lax.broadcasted_iota(jnp.int32, (8, LANES), 0) == 0
        absorb(new8[:, :kw], new8[:, kw:], first)
        l8 = jnp.concatenate([l_ref[...]] * 8, axis=0)
        o0 = jnp.sum(acc0_ref[...], axis=0, keepdims=True) / _dot2(l8, e0_ref[...])[0:1]
        o1 = jnp.sum(acc1_ref[...], axis=0, keepdims=True) / _dot2(l8, e1_ref[...])[0:1]
        lam = _diff_lambda(lam_ref, lam_init)
        for h in range(DIFF_HEADS):
            sl = slice(h * HEAD_DIM, (h + 1) * HEAD_DIM)
            out_ref[0, :, sl] = _diff_finish(o0[:, sl], o1[:, sl], lam, g_ref[...], lam_init)


def _diff_sample(table, qbd, new_rows, lam_rows, norm_g, cache, layer, lam_init):
    nb, npg = table.shape
    flat_rows = cache.shape[2]
    rpp = flat_rows // (2 * DIFF_HEADS)
    width = new_rows.shape[2]
    assert npg % PAGES_PER_STEP == 0 and (PAGES_PER_STEP * rpp) % DIFF_DEC_KB == 0
    n_chunks = npg // PAGES_PER_STEP
    kw = DIFF_HEADS * HEAD_DIM
    both = np.arange(LANES).reshape(LANES, 1)
    col_head = np.arange(kw).reshape(1, kw) // HEAD_DIM
    e0 = jnp.asarray(both == 2 * col_head, dtype=MXU_DTYPE)
    e1 = jnp.asarray(both == 2 * col_head + 1, dtype=MXU_DTYPE)
    return pl.pallas_call(
        functools.partial(_diff_sample_kernel, layer=layer, npg=npg, n_chunks=n_chunks, lam_init=lam_init),
        grid_spec=pltpu.PrefetchScalarGridSpec(
            num_scalar_prefetch=1,
            grid=(nb, n_chunks),
            in_specs=[pl.BlockSpec((1,) + qbd.shape[1:], lambda b, c, pt: (b, 0, 0)),
                      pl.BlockSpec(e0.shape, lambda b, c, pt: (0, 0)),
                      pl.BlockSpec(e1.shape, lambda b, c, pt: (0, 0)),
                      pl.BlockSpec((1, 1, width), lambda b, c, pt: (b, 0, 0)),
                      pl.BlockSpec(lam_rows.shape, lambda b, c, pt: (0, 0)),
                      pl.BlockSpec((1, HEAD_DIM), lambda b, c, pt: (0, 0)),
                      pl.BlockSpec(memory_space=pl.ANY)],
            out_specs=pl.BlockSpec((1, 1, kw), lambda b, c, pt: (b, 0, 0)),
            scratch_shapes=[pltpu.VMEM((2, PAGES_PER_STEP * flat_rows, HEAD_DIM), F32),
                            pltpu.SemaphoreType.DMA((2,)),
                            pltpu.VMEM((1, LANES), F32), pltpu.VMEM((1, LANES), F32),
                            pltpu.VMEM((8, kw), F32), pltpu.VMEM((8, kw), F32)]),
        out_shape=jax.ShapeDtypeStruct((nb, 1, kw), F32),
        compiler_params=_params("arbitrary", "arbitrary"),
        name="diff_sample",
    )(table.reshape(-1), qbd, e0, e1, new_rows, lam_rows, norm_g.reshape(1, HEAD_DIM), cache)


SAMPLE_ROWS = 16


def _pick_tm(m, cap):
    tm = min(m, cap)
    while m % tm:
        tm //= 2
    return tm


def kernel(x_prompt, x_sample, cache_nsa_cmp_kv, cache_nsa_slc_kv, state_nsa_win_kv, cache_sb_kv,
           cache_diff_kv, page_table, w_in, w_o, ln1_g, ln1_b, ln2_g, ln2_b, w_up, w_down,
           nsa_cmp_pos, nsa_cmp_w1, nsa_cmp_w2, diff_lambda, diff_norm_g):
    depth = w_in.shape[0]
    batch, seq, d_model = x_prompt.shape
    nbs, dec_seq, _ = x_sample.shape
    assert dec_seq == 1 and nbs <= SAMPLE_ROWS
    npg = page_table.shape[1]
    n_pool, page = cache_nsa_cmp_kv.shape[1], cache_nsa_cmp_kv.shape[2]
    past = npg * page
    alpha = (2 * depth) ** 0.25
    width = 2 * KV_WIDTH
    mp = batch * seq
    tm_p = _pick_tm(seq, 1024)

    tabs_p = (_rope_tables(jnp.arange(seq), HEAD_DIM, HEAD_DIM // ROT_FRACTION)
              + _rope_tables(jnp.arange(seq), DIFF_DIM, DIFF_DIM // ROT_FRACTION))
    pos_s = jnp.full((SAMPLE_ROWS,), past, jnp.int32)
    tabs_s = (_rope_tables(pos_s, HEAD_DIM, HEAD_DIM // ROT_FRACTION)
              + _rope_tables(pos_s, DIFF_DIM, DIFF_DIM // ROT_FRACTION))

    cmp_cache = cache_nsa_cmp_kv.reshape(depth, n_pool, page * SLABS, HEAD_DIM)
    slc_cache = cache_nsa_slc_kv.reshape(depth, n_pool, page * SLABS, HEAD_DIM)
    sb_cache = cache_sb_kv.reshape(depth, n_pool, page * 2 * SB_HEADS, HEAD_DIM)
    diff_cache = cache_diff_kv.reshape(depth, n_pool, page * 2 * DIFF_HEADS, HEAD_DIM)
    win_state = state_nsa_win_kv.reshape(depth, nbs, state_nsa_win_kv.shape[2] * SLABS, HEAD_DIM)
    prompt_pages = jnp.arange(mp // page, dtype=jnp.int32).reshape(1, mp // page)
    new_pages = jnp.arange(nbs, dtype=jnp.int32).reshape(1, nbs)
    w_t = _arrange_w_in(w_in)
    wo_b, wu_b, wd_b = w_o.astype(MXU_DTYPE), w_up.astype(MXU_DTYPE), w_down.astype(MXU_DTYPE)

    xp = x_prompt.reshape(mp, d_model)
    xs = jnp.pad(x_sample.reshape(nbs, d_model), ((0, SAMPLE_ROWS - nbs), (0, 0)))
    xp_b, xs_b = xp.astype(MXU_DTYPE), xs.astype(MXU_DTYPE)
    st_p = [[] for _ in range(5)]
    st_s = [[] for _ in range(5)]

    for l in range(depth):
        lam_init = 0.8 - 0.6 * math.exp(-0.3 * l)
        posc, w1bd, w2bd = _compress_weights(nsa_cmp_pos[l], nsa_cmp_w1[l], nsa_cmp_w2[l])
        lam_rows = diff_lambda[l].astype(F32)

        (qraw, qrot, cmp, slc, win, sbq, sbkv, dq, dkv, mg, ng) = _project(xp_b, w_t, l, tabs_p, tm_p, seq)
        kcv = _compress(cmp.reshape(1, mp // page, page * SLABS, HEAD_DIM), 0, prompt_pages, posc, w1bd, w2bd)
        o_nsa = _nsa_prompt(qraw, qrot, kcv, slc, win, ng, batch, seq)
        o_sb = _sb_prompt(sbq, sbkv, batch, seq)
        o_diff = _diff_prompt(dq, dkv, lam_rows, diff_norm_g[l], lam_init, batch, seq)
        x1, x1b = _merge(o_nsa, o_sb, o_diff, mg, xp, wo_b, l, ln1_g[l], ln1_b[l], alpha, _pick_tm(mp, 256))
        xp, xp_b = _ffn(x1, x1b, wu_b, wd_b, l, ln2_g[l], ln2_b[l], alpha, _pick_tm(mp, 512))
        win_rows = min(WINDOW, seq)
        st_p[0].append(cmp.reshape(batch, seq, 2, NSA_KV_HEADS, HEAD_DIM))
        st_p[1].append(slc.reshape(batch, seq, 2, NSA_KV_HEADS, HEAD_DIM))
        st_p[2].append(win.reshape(batch, seq, 2, NSA_KV_HEADS, HEAD_DIM)[:, seq - win_rows:])
        st_p[3].append(sbkv.reshape(batch, seq, 2, SB_HEADS, HEAD_DIM))
        st_p[4].append(dkv.reshape(batch, seq, 2, DIFF_HEADS, HEAD_DIM))

        (qraw, qrot, cmp, slc, win, sbq, sbkv, dq, dkv, mg, ng) = _project(
            xs_b, w_t, l, tabs_s, SAMPLE_ROWS, SAMPLE_ROWS)
        cmp, slc, win, sbkv, dkv = (a.reshape(SAMPLE_ROWS, -1) for a in (cmp, slc, win, sbkv, dkv))
        kcv_past = _compress(cmp_cache, l, page_table, posc, w1bd, w2bd)
        new_blk = jnp.zeros((1, nbs, NSA_BLOCK, width), F32).at[0, :, 0, :].set(cmp[:nbs])
        kcv_new = _compress(new_blk.reshape(1, nbs, NSA_BLOCK * SLABS, HEAD_DIM), 0, new_pages, posc, w1bd, w2bd)
        n_past = kcv_past.shape[1]
        n_rows = -(-(n_past + 1) // 8) * 8
        kcv_all = jnp.concatenate([kcv_past, kcv_new.reshape(nbs, 1, width),
                                   jnp.zeros((nbs, n_rows - n_past - 1, width), F32)], axis=1)
        scale = HEAD_DIM ** -0.5
        q4 = qraw[:nbs].reshape(nbs, NSA_KV_HEADS, NSA_GROUP, HEAD_DIM).transpose(0, 2, 1, 3)
        qsel = jnp.stack([_block_diag_q(q4[:, g].reshape(nbs, KV_WIDTH), NSA_KV_HEADS, HEAD_DIM, scale)
                          for g in range(NSA_GROUP)], axis=1)
        o_c, idx = _select(qsel, kcv_all, past)
        qr = qrot[:nbs].reshape(nbs, NSA_KV_HEADS, NSA_GROUP, HEAD_DIM).astype(F32) * scale
        qslc = jnp.pad(qr.transpose(0, 1, 3, 2), ((0, 0), (0, 0), (0, 0), (0, LANES - NSA_GROUP))).astype(MXU_DTYPE)
        o_nsa = _nsa_sample(idx, page_table, qslc, o_c, slc[:nbs].reshape(nbs, 1, width),
                            win[:nbs].reshape(nbs, 1, width), win_state, ng[:nbs].reshape(nbs, 1, COL_TILE),
                            slc_cache, l)
        o_sb = _sb_sample(page_table, _block_diag_q(sbq[:nbs], SB_HEADS, HEAD_DIM, scale), sb_cache, l)
        o_diff = _diff_sample(page_table, _block_diag_q(dq[:nbs], 2 * DIFF_HEADS, DIFF_DIM, DIFF_DIM ** -0.5),
                              dkv[:nbs].reshape(nbs, 1, -1), lam_rows, diff_norm_g[l], diff_cache, l, lam_init)

        def pad_rows(a):
            return jnp.pad(a.reshape(nbs, -1), ((0, SAMPLE_ROWS - nbs), (0, 0)))

        x1, x1b = _merge(pad_rows(o_nsa), pad_rows(o_sb), pad_rows(o_diff), mg, xs, wo_b, l, ln1_g[l], ln1_b[l],
                         alpha, SAMPLE_ROWS)
        xs, xs_b = _ffn(x1, x1b, wu_b, wd_b, l, ln2_g[l], ln2_b[l], alpha, SAMPLE_ROWS)
        new_win = win[:nbs].reshape(nbs, 1, 2, NSA_KV_HEADS, HEAD_DIM)
        wb = state_nsa_win_kv.shape[2]
        keep = min(WINDOW, past + 1)
        st_s[0].append(cmp[:nbs].reshape(nbs, 1, 2, NSA_KV_HEADS, HEAD_DIM))
        st_s[1].append(slc[:nbs].reshape(nbs, 1, 2, NSA_KV_HEADS, HEAD_DIM))
        st_s[2].append(jnp.concatenate([state_nsa_win_kv[l], new_win], axis=1)[:, wb + 1 - keep:])
        st_s[3].append(sbkv[:nbs].reshape(nbs, 1, 2, SB_HEADS, HEAD_DIM))
        st_s[4].append(dkv[:nbs].reshape(nbs, 1, 2, DIFF_HEADS, HEAD_DIM))

    return (xp.reshape(batch, seq, d_model), xs[:nbs].reshape(nbs, 1, d_model),
            jnp.stack(st_p[0]), jnp.stack(st_s[0]),
            jnp.stack(st_p[1]), jnp.stack(st_s[1]),
            jnp.stack(st_p[2]), jnp.stack(st_s[2]),
            jnp.stack(st_p[3]), jnp.stack(st_s[3]),
            jnp.stack(st_p[4]), jnp.stack(st_s[4]))
```

```python
import functools
import math

import numpy as np
import jax
import jax.numpy as jnp
from jax import lax
from jax.experimental import pallas as pl
from jax.experimental.pallas import tpu as pltpu

HEAD_DIM = 128
NSA_HEADS = 8
NSA_KV_HEADS = 2
NSA_GROUP = NSA_HEADS // NSA_KV_HEADS
NSA_BLOCK = 64
NSA_TOPN = 16
WINDOW = 512
SB_HEADS = 4
DIFF_HEADS = 4
DIFF_DIM = HEAD_DIM // 2
ROPE_THETA = 500000.0
ROT_FRACTION = 4
LN_EPS = 1e-5
NEG_INF = -1e30

F32 = jnp.float32
MXU_DTYPE = jnp.bfloat16

LANES = 128
COL_TILE = 256
VMEM_LIMIT_BYTES = 56 * 2**20

KV_WIDTH = NSA_KV_HEADS * HEAD_DIM
Q_TILES = NSA_HEADS * HEAD_DIM // COL_TILE
T_CMP = Q_TILES
T_SLC = T_CMP + 2
T_WIN = T_SLC + 2
T_SBQ = T_WIN + 2
T_SBKV = T_SBQ + SB_HEADS * HEAD_DIM // COL_TILE
T_DQ = T_SBKV + 2 * SB_HEADS * HEAD_DIM // COL_TILE
T_DKV = T_DQ + DIFF_HEADS * HEAD_DIM // COL_TILE
T_MG = T_DKV + 2 * DIFF_HEADS * HEAD_DIM // COL_TILE
MIX_WIDTH = (NSA_HEADS + SB_HEADS + DIFF_HEADS) * HEAD_DIM
T_NG = T_MG + MIX_WIDTH // COL_TILE
N_TILES = T_NG + 1


def _dot(a, b):
    return jnp.dot(a, b, preferred_element_type=F32)


def _dot_nt(a, b):
    return lax.dot_general(a, b, (((1,), (1,)), ((), ())), preferred_element_type=F32)


def _split(x):
    hi = x.astype(MXU_DTYPE)
    lo = (x - hi.astype(F32)).astype(MXU_DTYPE)
    return hi, lo


def _dot2(x, w):
    hi, lo = _split(x)
    return _dot(hi, w) + _dot(lo, w)


def _params(*sem):
    return pltpu.CompilerParams(dimension_semantics=sem, vmem_limit_bytes=VMEM_LIMIT_BYTES)


def _softplus(z):
    return jnp.maximum(z, 0.0) + jnp.log(1.0 + jnp.exp(-jnp.abs(z)))


def _layer_norm(h, g, b):
    mu = jnp.mean(h, axis=-1, keepdims=True)
    d = h - mu
    var = jnp.mean(d * d, axis=-1, keepdims=True)
    return d * lax.rsqrt(var + LN_EPS) * g + b


def _rope_tables(pos, width, rot_dim):
    half = rot_dim // 2
    inv_freq = ROPE_THETA ** (-jnp.arange(half, dtype=F32) / half)
    ang = pos.astype(F32)[:, None] * inv_freq[None, :]
    cos, sin = jnp.cos(ang), jnp.sin(ang)
    n = pos.shape[0]
    rest = width - 2 * half
    c = jnp.concatenate([cos, cos, jnp.ones((n, rest), F32)], axis=1)
    s_up = jnp.concatenate([jnp.zeros((n, half), F32), sin, jnp.zeros((n, rest), F32)], axis=1)
    s_dn = jnp.concatenate([-sin, jnp.zeros((n, half + rest), F32)], axis=1)
    rep = LANES // width
    c, s_up, s_dn = (jnp.tile(a, (1, rep)) for a in (c, s_up, s_dn))
    return (jnp.stack([c, jnp.ones_like(c)]), jnp.stack([s_up, jnp.zeros_like(c)]),
            jnp.stack([s_dn, jnp.zeros_like(c)]))


def _rope(a, c_ref, su_ref, sd_ref, shift):
    def one(h):
        return (h * c_ref[...] + pltpu.roll(h, shift, 1) * su_ref[...]
                + pltpu.roll(h, LANES - shift, 1) * sd_ref[...])
    return jnp.concatenate([one(a[:, :LANES]), one(a[:, LANES:])], axis=1)


def _put_slabs(ref, tile, val):
    rows = val.shape[0]
    slabs = ref.shape[0] // rows
    for k in range(COL_TILE // LANES):
        ref[pl.ds(2 * tile + k, rows, stride=slabs), :] = val[:, k * LANES:(k + 1) * LANES]


def _proj_kernel(x_ref, w_ref, *refs, kind, shift):
    acc = _dot_nt(x_ref[...], w_ref[...])
    if kind == "q":
        c_ref, su_ref, sd_ref, raw_ref, rot_ref = refs
        raw_ref[...] = acc.astype(raw_ref.dtype)
        rot_ref[...] = _rope(acc, c_ref, su_ref, sd_ref, shift).astype(rot_ref.dtype)
    elif kind == "kv":
        _put_slabs(refs[0], pl.program_id(1), acc)
    elif kind == "kv_rope":
        c_ref, su_ref, sd_ref, out_ref = refs
        _put_slabs(out_ref, pl.program_id(1), _rope(acc, c_ref, su_ref, sd_ref, shift))
    elif kind == "cast":
        refs[0][...] = acc.astype(refs[0].dtype)
    elif kind == "cast_rope":
        c_ref, su_ref, sd_ref, out_ref = refs
        out_ref[...] = _rope(acc, c_ref, su_ref, sd_ref, shift).astype(out_ref.dtype)
    else:
        assert kind == "sigmoid"
        refs[0][...] = jax.nn.sigmoid(acc)


def _proj_group(kind, xb, w_t, layer, tile0, n_tiles, tm, rows_per_seq, tabs=None, n_rope=0, shift=0,
                dtype=F32, slabs=0):
    m, d = xb.shape
    seq_tiles = rows_per_seq // tm
    in_specs = [pl.BlockSpec((tm, d), lambda i, n: (i, 0)),
                pl.BlockSpec((None, COL_TILE, d), lambda i, n: (layer, tile0 + n, 0))]
    args = [xb, w_t]
    if tabs is not None:
        tab_spec = pl.BlockSpec((None, tm, LANES),
                                lambda i, n: (jnp.where(n < n_rope, 0, 1), i % seq_tiles, 0))
        in_specs += [tab_spec] * 3
        args += list(tabs)
    if slabs:
        shape = jax.ShapeDtypeStruct((m * slabs, HEAD_DIM), F32)
        spec = pl.BlockSpec((tm * slabs, HEAD_DIM), lambda i, n: (i, 0))
    else:
        shape = jax.ShapeDtypeStruct((m, n_tiles * COL_TILE), dtype)
        spec = pl.BlockSpec((tm, COL_TILE), lambda i, n: (i, n))
    n_out = 2 if kind == "q" else 1
    out = pl.pallas_call(
        functools.partial(_proj_kernel, kind=kind, shift=shift),
        grid=(m // tm, n_tiles),
        in_specs=in_specs,
        out_specs=[spec] * n_out,
        out_shape=[shape] * n_out,
        compiler_params=_params("arbitrary", "arbitrary"),
        name="proj_" + kind,
    )(*args)
    return out if n_out > 1 else out[0]


def _project(xb, w_t, layer, tabs, tm, rows_per_seq):
    head_tabs, diff_tabs = tabs[:3], tabs[3:]
    s_head = HEAD_DIM // ROT_FRACTION // 2
    s_diff = DIFF_DIM // ROT_FRACTION // 2

    def group(kind, tile0, n_tiles, **kw):
        return _proj_group(kind, xb, w_t, layer, tile0, n_tiles, tm, rows_per_seq, **kw)

    qraw, qrot = group("q", 0, Q_TILES, tabs=head_tabs, n_rope=Q_TILES, shift=s_head, dtype=MXU_DTYPE)
    cmp = group("kv", T_CMP, 2, slabs=SLABS)
    slc = group("kv_rope", T_SLC, 2, tabs=head_tabs, n_rope=1, shift=s_head, slabs=SLABS)
    win = group("kv_rope", T_WIN, 2, tabs=head_tabs, n_rope=1, shift=s_head, slabs=SLABS)
    sbq = group("cast", T_SBQ, T_SBKV - T_SBQ, dtype=MXU_DTYPE)
    sbkv = group("kv", T_SBKV, T_DQ - T_SBKV, slabs=2 * SB_HEADS)
    n_dq = T_DKV - T_DQ
    dq = group("cast_rope", T_DQ, n_dq, tabs=diff_tabs, n_rope=n_dq, shift=s_diff, dtype=MXU_DTYPE)
    n_dkv = T_MG - T_DKV
    dkv = group("kv_rope", T_DKV, n_dkv, tabs=diff_tabs, n_rope=n_dkv // 2, shift=s_diff,
                slabs=2 * DIFF_HEADS)
    mg = group("sigmoid", T_MG, T_NG - T_MG)
    ng = group("sigmoid", T_NG, 1)
    return qraw, qrot, cmp, slc, win, sbq, sbkv, dq, dkv, mg, ng


def _arrange_w_in(w):
    wt = jnp.swapaxes(w, 1, 2)
    depth, _, d = wt.shape
    g0 = Q_TILES * COL_TILE + 6 * KV_WIDTH
    ng = 3 * NSA_HEADS
    per = ng // NSA_KV_HEADS
    pad = jnp.zeros((depth, LANES - per, d), w.dtype)
    return jnp.concatenate([wt[:, :g0], wt[:, g0 + ng:], wt[:, g0:g0 + per], pad,
                            wt[:, g0 + per:g0 + ng], pad], axis=1).astype(MXU_DTYPE)


R_CHUNK = 16
SLABS = 2 * NSA_KV_HEADS
SUBLANES = 8
BLK_SLOT = R_CHUNK * SLABS + SUBLANES


def _slab_rows(ref, first, count, stride, slabs):
    return jnp.concatenate([ref[pl.ds(first + c, count, stride=stride), :] for c in slabs], axis=1)


def _gelu(x):
    return 0.5 * x * (1.0 + jnp.tanh(math.sqrt(2.0 / math.pi) * (x + 0.044715 * (x * x * x))))


def _compress_kernel(pt_ref, pos_ref, w1_ref, w2_ref, cache_ref, out_ref, buf, sem, acc_ref,
                     *, layer, npg, bpp, n_steps):
    b = pl.program_id(0)
    j = pl.program_id(1)
    n_chunks = NSA_BLOCK // R_CHUNK
    step = b * n_chunks + j
    slot = step % 2
    nblk = npg * bpp

    def copy(bb, jj, sl, i, h):
        page = pt_ref[bb * npg + i]
        src = cache_ref.at[layer, page, pl.ds((h * NSA_BLOCK + jj * R_CHUNK) * SLABS, R_CHUNK * SLABS), :]
        dst = buf.at[sl, pl.ds((i * bpp + h) * BLK_SLOT, R_CHUNK * SLABS), :]
        return pltpu.make_async_copy(src, dst, sem.at[sl])

    def for_all(bb, jj, sl, start):
        def body(i, carry):
            for h in range(bpp):
                cp = copy(bb, jj, sl, i, h)
                if start:
                    cp.start(priority=h % 2)
                else:
                    cp.wait()
            return carry
        lax.fori_loop(0, npg, body, 0)

    @pl.when(step == 0)
    def _():
        for_all(b, j, slot, True)

    @pl.when(step + 1 < n_steps)
    def _():
        nxt = step + 1
        for_all(nxt // n_chunks, nxt % n_chunks, 1 - slot, True)

    for_all(b, j, slot, False)

    @pl.when(j == 0)
    def _():
        acc_ref[...] = jnp.zeros_like(acc_ref)

    half = 2 * HEAD_DIM
    a_k = jnp.zeros((nblk, half), F32)
    a_v = jnp.zeros((nblk, half), F32)
    for r in range(R_CHUNK):
        xr = _slab_rows(buf.at[slot], r * SLABS, nblk, BLK_SLOT, range(SLABS))
        xr = xr + pos_ref[pl.ds(j * R_CHUNK + r, 1), :]
        xb = xr.astype(MXU_DTYPE)
        a_k = a_k + _dot(xb[:, :half], w1_ref[r, 0])
        a_v = a_v + _dot(xb[:, half:], w1_ref[r, 1])
    acc_ref[:, :half] += a_k
    acc_ref[:, half:] += a_v

    @pl.when(j == n_chunks - 1)
    def _():
        h = _gelu(acc_ref[...]).astype(MXU_DTYPE)
        out_ref[0] = _dot(h, w2_ref[...])


def _compress(cache, layer, table, posc, w1bd, w2bd):
    nb, npg = table.shape
    rpp = cache.shape[2] // SLABS
    bpp = rpp // NSA_BLOCK
    nblk = npg * bpp
    n_chunks = NSA_BLOCK // R_CHUNK
    width = 2 * KV_WIDTH
    kern = functools.partial(_compress_kernel, layer=layer, npg=npg, bpp=bpp, n_steps=nb * n_chunks)
    return pl.pallas_call(
        kern,
        grid_spec=pltpu.PrefetchScalarGridSpec(
            num_scalar_prefetch=1,
            grid=(nb, n_chunks),
            in_specs=[pl.BlockSpec((NSA_BLOCK, width), lambda b, j, pt: (0, 0)),
                      pl.BlockSpec((R_CHUNK,) + w1bd.shape[1:], lambda b, j, pt: (j, 0, 0, 0)),
                      pl.BlockSpec(w2bd.shape, lambda b, j, pt: (0, 0)),
                      pl.BlockSpec(memory_space=pl.ANY)],
            out_specs=pl.BlockSpec((1, nblk, width), lambda b, j, pt: (b, 0, 0)),
            scratch_shapes=[pltpu.VMEM((2, nblk * BLK_SLOT, HEAD_DIM), F32),
                            pltpu.SemaphoreType.DMA((2,)),
                            pltpu.VMEM((nblk, width), F32)]),
        out_shape=jax.ShapeDtypeStruct((nb, nblk, width), F32),
        compiler_params=_params("arbitrary", "arbitrary"),
        name="nsa_compress",
    )(table.reshape(-1), posc, w1bd, w2bd, cache)


def _compress_weights(pos, w1, w2):
    posc = jnp.concatenate([pos[0]] * NSA_KV_HEADS + [pos[1]] * NSA_KV_HEADS, axis=1)
    w1r = w1.reshape(2, NSA_BLOCK, HEAD_DIM, HEAD_DIM)
    z = jnp.zeros_like(w1r)
    w1bd = jnp.concatenate([jnp.concatenate([w1r, z], axis=3), jnp.concatenate([z, w1r], axis=3)], axis=2)
    w1bd = jnp.transpose(w1bd, (1, 0, 2, 3)).astype(MXU_DTYPE)
    blocks = [w2[0]] * NSA_KV_HEADS + [w2[1]] * NSA_KV_HEADS
    n = len(blocks)
    w2bd = jnp.concatenate(
        [jnp.concatenate([blk if c == r else jnp.zeros_like(blk) for c in range(n)], axis=1)
         for r, blk in enumerate(blocks)], axis=0).astype(MXU_DTYPE)
    return posc, w1bd, w2bd


NSA_TQ = 256
NSA_KB = 256


def _fill_kv_scratch(kv_ref, head, n_heads, kb_ref, vt_ref, seq, chunk):
    slabs = 2 * n_heads
    for c in range(seq // chunk):
        rows = slice(c * chunk, (c + 1) * chunk)
        base = c * chunk * slabs
        kb_ref[rows, :] = kv_ref[pl.ds(base + head, chunk, stride=slabs), :].astype(MXU_DTYPE)
        vt_ref[:, rows] = kv_ref[pl.ds(base + n_heads + head, chunk, stride=slabs), :].T.astype(MXU_DTYPE)


def _online_softmax_step(xm, vt, m_ref, l_ref, acc_ref):
    m_old = m_ref[...]
    m_new = jnp.maximum(m_old, jnp.max(xm, axis=0, keepdims=True))
    alpha = jnp.exp(m_old - m_new)
    p = jnp.exp(xm - jnp.maximum(m_new, 0.1 * NEG_INF))
    l_ref[...] = alpha * l_ref[...] + jnp.sum(p, axis=0, keepdims=True)
    acc_ref[...] = alpha * acc_ref[...] + _dot(vt, p.astype(MXU_DTYPE))
    m_ref[...] = m_new


def _nsa_prompt_kernel(qraw_ref, qrot_ref, kc_ref, vc_ref, slc_ref, win_ref, ng_ref,
                       eblk_ref, out_ref, skb_ref, svt_ref, wkb_ref, wvt_ref, m_ref, l_ref, acc_ref, s_ref,
                       *, seq, nb):
    head = pl.program_id(1)
    qi = pl.program_id(2)
    tq = NSA_TQ
    g4 = NSA_GROUP
    nq = g4 * tq
    t0 = qi * tq
    scale = HEAD_DIM ** -0.5
    n_sel = min(NSA_TOPN, nb)

    @pl.when(qi == 0)
    def _():
        _fill_kv_scratch(slc_ref, head, NSA_KV_HEADS, skb_ref, svt_ref, seq, NSA_KB)
        _fill_kv_scratch(win_ref, head, NSA_KV_HEADS, wkb_ref, wvt_ref, seq, NSA_KB)

    def stack(ref):
        return jnp.concatenate([ref[:, g * HEAD_DIM:(g + 1) * HEAD_DIM] for g in range(g4)], axis=0)

    def tile_lanes(a):
        return jnp.concatenate([a] * g4, axis=1)

    qraw = stack(qraw_ref)
    qrot = stack(qrot_ref)
    t_q = t0 + lax.broadcasted_iota(jnp.int32, (1, tq), 1)
    t_lane = tile_lanes(t_q)

    kc = kc_ref[...].astype(MXU_DTYPE)
    s_c = (_dot_nt(kc, qraw) * scale)[0:nb]
    blk = lax.broadcasted_iota(jnp.int32, (nb, 1), 0)
    vis_c = ((blk + 1) * NSA_BLOCK - 1) <= t_lane
    x_c = jnp.where(vis_c, s_c, NEG_INF)
    e_c = jnp.where(vis_c, jnp.exp(x_c - jnp.max(x_c, axis=0, keepdims=True)), 0.0)
    den = jnp.sum(e_c, axis=0, keepdims=True)
    p_c = e_c / jnp.where(den > 0.0, den, 1.0)
    vc_t = vc_ref[...].T[:, 0:nb].astype(MXU_DTYPE)
    o_c = _dot(vc_t, p_c.astype(MXU_DTYPE))

    imp = p_c[:, 0:tq]
    for g in range(1, g4):
        imp = imp + p_c[:, g * tq:(g + 1) * tq]
    cur = t_q // NSA_BLOCK
    n_idx = lax.broadcasted_iota(jnp.int32, (nb, tq), 0)
    rank = jnp.zeros((nb, tq), jnp.int32)
    for mm in range(nb):
        row = imp[mm:mm + 1, :]
        ahead = jnp.where(row > imp, 1, jnp.where(row == imp, jnp.where(mm < n_idx, 1, 0), 0))
        rank = rank + jnp.where(mm < cur, ahead, 0)
    sel = jnp.where(n_idx == cur, 1.0, jnp.where(n_idx < cur, jnp.where(rank < n_sel - 1, 1.0, 0.0), 0.0))
    sel = jnp.concatenate([sel, jnp.zeros((LANES - nb, tq), F32)], axis=0).astype(MXU_DTYPE)

    m_ref[...] = jnp.full_like(m_ref, NEG_INF)
    l_ref[...] = jnp.zeros_like(l_ref)
    acc_ref[...] = jnp.zeros_like(acc_ref)
    n_chunks = (t0 + tq - 1) // NSA_KB + 1

    def slc_scores(kj):
        k0 = pl.multiple_of(kj * NSA_KB, NSA_KB)
        s = _dot_nt(skb_ref[pl.ds(k0, NSA_KB), :], qrot) * scale
        selm = tile_lanes(_dot(eblk_ref[kj], sel))
        kpos = k0 + lax.broadcasted_iota(jnp.int32, (NSA_KB, 1), 0)
        return jnp.where(selm > 0.5, jnp.where(kpos <= t_lane, s, NEG_INF), NEG_INF)

    s_ref[...] = slc_scores(0)

    def slc_body(kj, carry):
        xm = s_ref[...]
        nxt = slc_scores(jnp.minimum(kj + 1, n_chunks - 1))
        k0 = pl.multiple_of(kj * NSA_KB, NSA_KB)
        _online_softmax_step(xm, svt_ref[:, pl.ds(k0, NSA_KB)], m_ref, l_ref, acc_ref)
        s_ref[...] = nxt
        return carry

    lax.fori_loop(0, n_chunks, slc_body, 0)
    o_s = acc_ref[...] / l_ref[...]

    band = min(WINDOW + tq, seq)
    w0 = pl.multiple_of(jnp.clip(t0 + tq - band, 0, seq - band), tq)
    s_w = _dot_nt(wkb_ref[pl.ds(w0, band), :], qrot) * scale
    dpos = t_lane - (w0 + lax.broadcasted_iota(jnp.int32, (band, 1), 0))
    x_w = jnp.where(dpos >= 0, jnp.where(dpos < WINDOW, s_w, NEG_INF), NEG_INF)
    e_w = jnp.exp(x_w - jnp.max(x_w, axis=0, keepdims=True))
    o_w = _dot(wvt_ref[:, pl.ds(w0, band)], e_w.astype(MXU_DTYPE)) / jnp.sum(e_w, axis=0, keepdims=True)

    gates = ng_ref[...].T
    for g in range(g4):
        r = slice(g * tq, (g + 1) * tq)
        o = (gates[3 * g:3 * g + 1] * o_c[:, r] + gates[3 * g + 1:3 * g + 2] * o_s[:, r]
             + gates[3 * g + 2:3 * g + 3] * o_w[:, r])
        out_ref[:, g * HEAD_DIM:(g + 1) * HEAD_DIM] = o.T


def _nsa_prompt(qraw, qrot, kcv, slc, win, ng, batch, seq):
    m = qraw.shape[0]
    nb = seq // NSA_BLOCK
    assert nb <= LANES and seq % NSA_KB == 0 and seq % NSA_TQ == 0
    nq = seq // NSA_TQ
    nkc = seq // NSA_KB
    kcv_p = jnp.pad(kcv.reshape(batch, nb, 2 * KV_WIDTH), ((0, 0), (0, LANES - nb), (0, 0)))
    kpos = np.arange(seq).reshape(nkc, NSA_KB, 1) // NSA_BLOCK
    eblk = jnp.asarray(kpos == np.arange(LANES).reshape(1, 1, LANES), dtype=MXU_DTYPE)
    gw = NSA_GROUP * HEAD_DIM
    lanes = NSA_GROUP * NSA_TQ
    q_spec = pl.BlockSpec((NSA_TQ, gw), lambda b, h, q: (b * nq + q, h))

    kv_spec = pl.BlockSpec((seq * SLABS, HEAD_DIM), lambda b, h, q: (b, 0))

    return pl.pallas_call(
        functools.partial(_nsa_prompt_kernel, seq=seq, nb=nb),
        grid=(batch, NSA_KV_HEADS, nq),
        in_specs=[q_spec, q_spec,
                  pl.BlockSpec((None, LANES, HEAD_DIM), lambda b, h, q: (b, 0, h)),
                  pl.BlockSpec((None, LANES, HEAD_DIM), lambda b, h, q: (b, 0, NSA_KV_HEADS + h)),
                  kv_spec, kv_spec,
                  pl.BlockSpec((NSA_TQ, LANES), lambda b, h, q: (b * nq + q, h)),
                  pl.BlockSpec(eblk.shape, lambda b, h, q: (0, 0, 0))],
        out_specs=pl.BlockSpec((NSA_TQ, gw), lambda b, h, q: (b * nq + q, h)),
        out_shape=jax.ShapeDtypeStruct((m, NSA_HEADS * HEAD_DIM), F32),
        scratch_shapes=[pltpu.VMEM((seq, HEAD_DIM), MXU_DTYPE), pltpu.VMEM((HEAD_DIM, seq), MXU_DTYPE),
                        pltpu.VMEM((seq, HEAD_DIM), MXU_DTYPE), pltpu.VMEM((HEAD_DIM, seq), MXU_DTYPE),
                        pltpu.VMEM((1, lanes), F32), pltpu.VMEM((1, lanes), F32),
                        pltpu.VMEM((HEAD_DIM, lanes), F32), pltpu.VMEM((NSA_KB, lanes), F32)],
        compiler_params=_params("arbitrary", "arbitrary", "arbitrary"),
        name="nsa_prompt",
    )(qraw, qrot, kcv_p, kcv_p, slc, win, ng, eblk)


SB_TQ = 512
SB_KB = 256


def _suffix_matrix_rows(kb):
    s = np.arange(kb + 8).reshape(kb + 8, 1)
    j = np.arange(kb).reshape(1, kb)
    return jnp.asarray(np.logical_or(j > s, s >= kb), dtype=MXU_DTYPE)


def _sb_prompt_kernel(q_ref, kv_ref, lw_ref, out_ref, kb_ref, vt_ref, tail_ref, acc_ref, *, seq):
    qi = pl.program_id(2)
    tq = SB_TQ
    t0 = qi * tq
    scale = HEAD_DIM ** -0.5
    kb = SB_KB

    @pl.when(qi == 0)
    def _():
        _fill_kv_scratch(kv_ref, pl.program_id(1), SB_HEADS, kb_ref, vt_ref, seq, kb)

    q = q_ref[...]
    t_lane = t0 + lax.broadcasted_iota(jnp.int32, (1, tq), 1)
    tail_ref[...] = jnp.zeros_like(tail_ref)
    acc_ref[...] = jnp.zeros_like(acc_ref)
    n_chunks = (t0 + tq) // kb

    def body(kk, carry):
        k0 = pl.multiple_of((n_chunks - 1 - kk) * kb, kb)
        z = _dot_nt(kb_ref[pl.ds(k0, kb), :], q) * scale
        valid = (k0 + lax.broadcasted_iota(jnp.int32, (kb, 1), 0)) < t_lane
        sp = _softplus(z)
        hi, lo = _split(jnp.where(valid, -sp, 0.0))
        sums = _dot(lw_ref[...], hi) + _dot(lw_ref[...], lo)
        tail = tail_ref[...]
        a = jnp.where(valid, jnp.exp(z - sp + sums[:kb] + tail), 0.0)
        acc_ref[...] += _dot(vt_ref[:, pl.ds(k0, kb)], a.astype(MXU_DTYPE))
        tail_ref[...] = tail + sums[kb:kb + 1]
        return carry

    lax.fori_loop(0, n_chunks, body, 0)
    out_ref[...] = acc_ref[...].T


def _sb_prompt(sbq, sbkv, batch, seq):
    m = sbq.shape[0]
    assert seq % SB_TQ == 0 and SB_TQ % SB_KB == 0
    nq = seq // SB_TQ
    lw = _suffix_matrix_rows(SB_KB)
    return pl.pallas_call(
        functools.partial(_sb_prompt_kernel, seq=seq),
        grid=(batch, SB_HEADS, nq),
        in_specs=[pl.BlockSpec((SB_TQ, HEAD_DIM), lambda b, h, q: (b * nq + q, h)),
                  pl.BlockSpec((seq * 2 * SB_HEADS, HEAD_DIM), lambda b, h, q: (b, 0)),
                  pl.BlockSpec(lw.shape, lambda b, h, q: (0, 0))],
        out_specs=pl.BlockSpec((SB_TQ, HEAD_DIM), lambda b, h, q: (b * nq + q, h)),
        out_shape=jax.ShapeDtypeStruct((m, SB_HEADS * HEAD_DIM), F32),
        scratch_shapes=[pltpu.VMEM((seq, HEAD_DIM), MXU_DTYPE), pltpu.VMEM((HEAD_DIM, seq), MXU_DTYPE),
                        pltpu.VMEM((1, SB_TQ), F32), pltpu.VMEM((HEAD_DIM, SB_TQ), F32)],
        compiler_params=_params("arbitrary", "arbitrary", "arbitrary"),
        name="sb_prompt",
    )(sbq, sbkv, lw)


DIFF_TQ = 512
DIFF_KB = 256


def _diff_lambda(lam_ref, lam_init):
    lv = lam_ref[...]
    s1 = jnp.sum(lv[0:1] * lv[1:2], axis=1, keepdims=True)
    s2 = jnp.sum(lv[2:3] * lv[3:4], axis=1, keepdims=True)
    return jnp.exp(s1) - jnp.exp(s2) + lam_init


def _diff_finish(o0, o1, lam, g, lam_init):
    o = o0 - lam * o1
    o = o * lax.rsqrt(jnp.mean(o * o, axis=-1, keepdims=True) + LN_EPS)
    return o * g * (1.0 - lam_init)


def _diff_prompt_kernel(q_ref, kv_ref, lam_ref, g_ref, out_ref, kb_ref, vt_ref, m_ref, l_ref, acc_ref, s_ref,
                        *, lam_init, seq):
    qi = pl.program_id(2)
    tq = DIFF_TQ
    t0 = qi * tq
    scale = DIFF_DIM ** -0.5
    kb = DIFF_KB

    @pl.when(qi == 0)
    def _():
        _fill_kv_scratch(kv_ref, pl.program_id(1), DIFF_HEADS, kb_ref, vt_ref, seq, kb)

    q = q_ref[...]
    lane = lax.broadcasted_iota(jnp.int32, (tq, HEAD_DIM), 1)
    zero = jnp.zeros_like(q)
    qq = jnp.concatenate([jnp.where(lane < DIFF_DIM, q, zero), jnp.where(lane >= DIFF_DIM, q, zero)], axis=0)
    t_q = t0 + lax.broadcasted_iota(jnp.int32, (1, tq), 1)
    t_lane = jnp.concatenate([t_q, t_q], axis=1)
    m_ref[...] = jnp.full_like(m_ref, NEG_INF)
    l_ref[...] = jnp.zeros_like(l_ref)
    acc_ref[...] = jnp.zeros_like(acc_ref)

    n_chunks = (t0 + tq) // kb

    def scores(kj):
        k0 = pl.multiple_of(kj * kb, kb)
        s = _dot_nt(kb_ref[pl.ds(k0, kb), :], qq) * scale
        kpos = k0 + lax.broadcasted_iota(jnp.int32, (kb, 1), 0)
        return jnp.where(kpos <= t_lane, s, NEG_INF)

    s_ref[...] = scores(0)

    def body(kj, carry):
        xm = s_ref[...]
        nxt = scores(jnp.minimum(kj + 1, n_chunks - 1))
        k0 = pl.multiple_of(kj * kb, kb)
        _online_softmax_step(xm, vt_ref[:, pl.ds(k0, kb)], m_ref, l_ref, acc_ref)
        s_ref[...] = nxt
        return carry

    lax.fori_loop(0, n_chunks, body, 0)
    o = acc_ref[...] / l_ref[...]
    lam = _diff_lambda(lam_ref, lam_init)
    out_ref[...] = _diff_finish(o[:, :tq].T, o[:, tq:].T, lam, g_ref[...], lam_init)


def _diff_prompt(dq, dkv, lam_rows, norm_g, lam_init, batch, seq):
    m = dq.shape[0]
    assert seq % DIFF_TQ == 0 and DIFF_TQ % DIFF_KB == 0
    nq = seq // DIFF_TQ
    return pl.pallas_call(
        functools.partial(_diff_prompt_kernel, lam_init=lam_init, seq=seq),
        grid=(batch, DIFF_HEADS, nq),
        in_specs=[pl.BlockSpec((DIFF_TQ, HEAD_DIM), lambda b, h, q: (b * nq + q, h)),
                  pl.BlockSpec((seq * 2 * DIFF_HEADS, HEAD_DIM), lambda b, h, q: (b, 0)),
                  pl.BlockSpec(lam_rows.shape, lambda b, h, q: (0, 0)),
                  pl.BlockSpec((1, HEAD_DIM), lambda b, h, q: (0, 0))],
        out_specs=pl.BlockSpec((DIFF_TQ, HEAD_DIM), lambda b, h, q: (b * nq + q, h)),
        out_shape=jax.ShapeDtypeStruct((m, DIFF_HEADS * HEAD_DIM), F32),
        scratch_shapes=[pltpu.VMEM((seq, HEAD_DIM), MXU_DTYPE), pltpu.VMEM((HEAD_DIM, seq), MXU_DTYPE),
                        pltpu.VMEM((1, 2 * DIFF_TQ), F32), pltpu.VMEM((1, 2 * DIFF_TQ), F32),
                        pltpu.VMEM((HEAD_DIM, 2 * DIFF_TQ), F32), pltpu.VMEM((DIFF_KB, 2 * DIFF_TQ), F32)],
        compiler_params=_params("arbitrary", "arbitrary", "arbitrary"),
        name="diff_prompt",
    )(dq, dkv, lam_rows, norm_g.reshape(1, HEAD_DIM))


def _merge_kernel(on_ref, os_ref, od_ref, mg_ref, x_ref, wo_ref, g_ref, b_ref, y_ref, yb_ref, *, alpha):
    n1 = on_ref.shape[1]
    n2 = n1 + os_ref.shape[1]
    r = _dot((on_ref[...] * mg_ref[:, :n1]).astype(MXU_DTYPE), wo_ref[:n1, :])
    r = r + _dot((os_ref[...] * mg_ref[:, n1:n2]).astype(MXU_DTYPE), wo_ref[n1:n2, :])
    r = r + _dot((od_ref[...] * mg_ref[:, n2:]).astype(MXU_DTYPE), wo_ref[n2:, :])
    y = _layer_norm(alpha * x_ref[...] + r, g_ref[...], b_ref[...])
    y_ref[...] = y
    yb_ref[...] = y.astype(yb_ref.dtype)


def _merge(o_nsa, o_sb, o_diff, mg, x, wo, layer, g, b, alpha, tm):
    m, d = x.shape

    def row(width):
        return pl.BlockSpec((tm, width), lambda i: (i, 0))

    def full(a):
        return pl.BlockSpec(a.shape, lambda i: (0, 0))

    g2, b2 = g.reshape(1, d), b.reshape(1, d)
    return pl.pallas_call(
        functools.partial(_merge_kernel, alpha=alpha),
        grid=(m // tm,),
        in_specs=[row(o_nsa.shape[1]), row(o_sb.shape[1]), row(o_diff.shape[1]), row(mg.shape[1]), row(d),
                  pl.BlockSpec((None,) + wo.shape[1:], lambda i: (layer, 0, 0)), full(g2), full(b2)],
        out_specs=[row(d), row(d)],
        out_shape=[jax.ShapeDtypeStruct((m, d), F32), jax.ShapeDtypeStruct((m, d), MXU_DTYPE)],
        compiler_params=_params("arbitrary"),
        name="merge_out_proj",
    )(o_nsa, o_sb, o_diff, mg, x, wo, g2, b2)


FFN_TF = 1024


def _ffn_kernel(x_ref, xb_ref, wu_ref, wd_ref, g_ref, b_ref, y_ref, yb_ref, acc_ref, *, alpha):
    j = pl.program_id(1)

    @pl.when(j == 0)
    def _():
        acc_ref[...] = jnp.zeros_like(acc_ref)

    h = jnp.maximum(_dot(xb_ref[...], wu_ref[...]), 0.0)
    acc_ref[...] += _dot((h * h).astype(MXU_DTYPE), wd_ref[...])

    @pl.when(j == pl.num_programs(1) - 1)
    def _():
        y = _layer_norm(alpha * x_ref[...] + acc_ref[...], g_ref[...], b_ref[...])
        y_ref[...] = y
        yb_ref[...] = y.astype(yb_ref.dtype)


def _ffn(x, xb, wu, wd, layer, g, b, alpha, tm):
    m, d = x.shape
    dff = wu.shape[2]
    g2, b2 = g.reshape(1, d), b.reshape(1, d)
    return pl.pallas_call(
        functools.partial(_ffn_kernel, alpha=alpha),
        grid=(m // tm, dff // FFN_TF),
        in_specs=[pl.BlockSpec((tm, d), lambda i, j: (i, 0)),
                  pl.BlockSpec((tm, d), lambda i, j: (i, 0)),
                  pl.BlockSpec((None, d, FFN_TF), lambda i, j: (layer, 0, j)),
                  pl.BlockSpec((None, FFN_TF, d), lambda i, j: (layer, j, 0)),
                  pl.BlockSpec((1, d), lambda i, j: (0, 0)),
                  pl.BlockSpec((1, d), lambda i, j: (0, 0))],
        out_specs=[pl.BlockSpec((tm, d), lambda i, j: (i, 0)), pl.BlockSpec((tm, d), lambda i, j: (i, 0))],
        out_shape=[jax.ShapeDtypeStruct((m, d), F32), jax.ShapeDtypeStruct((m, d), MXU_DTYPE)],
        scratch_shapes=[pltpu.VMEM((tm, d), F32)],
        compiler_params=_params("arbitrary", "arbitrary"),
        name="ffn",
    )(x, xb, wu, wd, g2, b2)


def _expand_matrix(n_groups, width):
    c = np.arange(LANES).reshape(LANES, 1)
    j = np.arange(n_groups * width).reshape(1, -1) // width
    return jnp.asarray(c == j, dtype=MXU_DTYPE)


def _block_diag_q(q, n_heads, width, scale):
    b = q.shape[0]
    head = jnp.arange(n_heads * width) // width
    onehot = (head[:, None] == jnp.arange(LANES)[None, :]).astype(F32)
    return ((q.astype(F32) * scale)[:, :, None] * onehot[None]).astype(MXU_DTYPE)


def _select_kernel(qsel_ref, kcv_ref, e2_ref, oc_ref, idx_ref, *, pos_q, n_rows):
    half = KV_WIDTH
    kcv = kcv_ref[0]
    kc = kcv[:, :half].astype(MXU_DTYPE)
    vc = kcv[:, half:]
    n_idx = lax.broadcasted_iota(jnp.int32, (n_rows, LANES), 0)
    vis = ((n_idx + 1) * NSA_BLOCK - 1) <= pos_q
    imp = jnp.zeros((n_rows, LANES), F32)
    for g in range(NSA_GROUP):
        s = _dot(kc, qsel_ref[0, g])
        xm = jnp.where(vis, s, NEG_INF)
        mx = jnp.max(xm, axis=0, keepdims=True)
        e = jnp.where(vis, jnp.exp(xm - mx), 0.0)
        den = jnp.sum(e, axis=0, keepdims=True)
        p = e / jnp.where(den > 0.0, den, 1.0)
        imp = imp + p
        o = jnp.sum(_dot2(p, e2_ref[...]) * vc, axis=0, keepdims=True)
        for h in range(NSA_KV_HEADS):
            c = (h * NSA_GROUP + g) * HEAD_DIM
            oc_ref[0, :, c:c + HEAD_DIM] = o[:, h * HEAD_DIM:(h + 1) * HEAD_DIM]
    cur = pos_q // NSA_BLOCK
    n_f = n_idx.astype(F32)
    work = jnp.where(n_idx < cur, imp, -1.0)
    for k in range(NSA_TOPN - 1):
        mx = jnp.max(work, axis=0, keepdims=True)
        ix = jnp.min(jnp.where(work == mx, n_f, float(n_rows)), axis=0, keepdims=True)
        idx_ref[0, k:k + 1, :] = ix.astype(jnp.int32)
        work = jnp.where(n_f == ix, -2.0, work)
    idx_ref[0, NSA_TOPN - 1:NSA_TOPN, :] = jnp.full((1, LANES), cur, jnp.int32)


def _select(qsel, kcv, pos_q):
    nb, n_rows, width = kcv.shape
    e2 = _expand_matrix(NSA_KV_HEADS, HEAD_DIM)
    return pl.pallas_call(
        functools.partial(_select_kernel, pos_q=pos_q, n_rows=n_rows),
        grid=(nb,),
        in_specs=[pl.BlockSpec((1,) + qsel.shape[1:], lambda b: (b, 0, 0, 0)),
                  pl.BlockSpec((1, n_rows, width), lambda b: (b, 0, 0)),
                  pl.BlockSpec(e2.shape, lambda b: (0, 0))],
        out_specs=[pl.BlockSpec((1, 1, NSA_HEADS * HEAD_DIM), lambda b: (b, 0, 0)),
                   pl.BlockSpec((1, NSA_TOPN, LANES), lambda b: (b, 0, 0))],
        out_shape=[jax.ShapeDtypeStruct((nb, 1, NSA_HEADS * HEAD_DIM), F32),
                   jax.ShapeDtypeStruct((nb, NSA_TOPN, LANES), jnp.int32)],
        compiler_params=_params("arbitrary"),
        name="nsa_select",
    )(qsel, kcv, e2)


def _softmax_rows_plus_one(s, vis, s_new):
    xm = s if vis is None else jnp.where(vis, s, NEG_INF)
    mx = jnp.maximum(jnp.max(xm, axis=0, keepdims=True), s_new)
    e = jnp.exp(xm - mx)
    if vis is not None:
        e = jnp.where(vis, e, 0.0)
    e_new = jnp.exp(s_new - mx)
    den = jnp.sum(e, axis=0, keepdims=True) + e_new
    return e / den, e_new / den


def _nsa_sample_kernel(idx_ref, pt_ref, q_ref, oc_ref, slc_new_ref, win_new_ref, ws_ref, ng_ref, e4_ref,
                       cache_ref, out_ref, kvbuf, sem, *, layer, npg, wb):
    b = pl.program_id(0)
    n_old = NSA_TOPN - 1
    blk_rows = NSA_BLOCK * SLABS
    blocks_per_page = cache_ref.shape[2] // blk_rows

    def copies(fn):
        for h in range(NSA_KV_HEADS):
            for k in range(n_old):
                n = idx_ref[(b * NSA_TOPN + k) * NSA_KV_HEADS + h]
                page = pt_ref[b * npg + n // blocks_per_page]
                r0 = pl.multiple_of((n % blocks_per_page) * blk_rows, blk_rows)
                src = cache_ref.at[layer, page, pl.ds(r0, blk_rows), :]
                fn(pltpu.make_async_copy(src, kvbuf.at[h, pl.ds(k * blk_rows, blk_rows), :], sem.at[0]))

    copies(lambda cp: cp.start())
    copies(lambda cp: cp.wait())

    gw = NSA_GROUP * HEAD_DIM
    e4 = e4_ref[...]
    slc_new = slc_new_ref[0]
    win_new = win_new_ref[0]
    gates = ng_ref[0]
    n_keys = n_old * NSA_BLOCK
    i_w = lax.broadcasted_iota(jnp.int32, (wb, LANES), 0)
    vis_w = (wb - i_w) < WINDOW

    def attend(keys, vals, vis, k_new, v_new, qg):
        s = _dot(keys.astype(MXU_DTYPE), qg)
        s_new = _dot(jnp.concatenate([k_new] * 8, axis=0).astype(MXU_DTYPE), qg)[0:1]
        p, p_new = _softmax_rows_plus_one(s, vis, s_new)
        vt = jnp.concatenate([vals] * NSA_GROUP, axis=1)
        o = jnp.sum(_dot2(p, e4) * vt, axis=0, keepdims=True)
        o_new = _dot2(jnp.concatenate([p_new] * 8, axis=0), e4)[0:1] * jnp.concatenate([v_new] * NSA_GROUP, axis=1)
        return o + o_new

    for h in range(NSA_KV_HEADS):
        qg = q_ref[0, h]
        ks, vs = h * HEAD_DIM, KV_WIDTH + h * HEAD_DIM
        o_s = attend(_slab_rows(kvbuf.at[h], 0, n_keys, SLABS, [h]),
                     _slab_rows(kvbuf.at[h], 0, n_keys, SLABS, [NSA_KV_HEADS + h]), None,
                     slc_new[:, ks:ks + HEAD_DIM], slc_new[:, vs:vs + HEAD_DIM], qg)
        o_w = attend(_slab_rows(ws_ref.at[0], 0, wb, SLABS, [h]),
                     _slab_rows(ws_ref.at[0], 0, wb, SLABS, [NSA_KV_HEADS + h]), vis_w,
                     win_new[:, ks:ks + HEAD_DIM], win_new[:, vs:vs + HEAD_DIM], qg)
        o_c = oc_ref[0, :, h * gw:(h + 1) * gw]
        for g in range(NSA_GROUP):
            c = h * LANES + 3 * g
            sl = slice(g * HEAD_DIM, (g + 1) * HEAD_DIM)
            out_ref[0, :, h * gw + g * HEAD_DIM:h * gw + (g + 1) * HEAD_DIM] = (
                gates[:, c:c + 1] * o_c[:, sl] + gates[:, c + 1:c + 2] * o_s[:, sl]
                + gates[:, c + 2:c + 3] * o_w[:, sl])


def _nsa_sample(idx, table, qslc, o_c, slc_new, win_new, win_state, ng, cache, layer):
    nb, npg = table.shape
    wb = win_state.shape[2] // SLABS
    width = 2 * KV_WIDTH
    e4 = _expand_matrix(NSA_GROUP, HEAD_DIM)
    ws = win_state.reshape(win_state.shape[0] * nb, wb * SLABS, HEAD_DIM)
    idx_flat = idx[:, :, :NSA_KV_HEADS].reshape(-1)

    def one(width_):
        return pl.BlockSpec((1, 1, width_), lambda b, ix, pt: (b, 0, 0))

    return pl.pallas_call(
        functools.partial(_nsa_sample_kernel, layer=layer, npg=npg, wb=wb),
        grid_spec=pltpu.PrefetchScalarGridSpec(
            num_scalar_prefetch=2,
            grid=(nb,),
            in_specs=[pl.BlockSpec((1,) + qslc.shape[1:], lambda b, ix, pt: (b, 0, 0, 0)),
                      one(NSA_HEADS * HEAD_DIM), one(width), one(width),
                      pl.BlockSpec((1, wb * SLABS, HEAD_DIM), lambda b, ix, pt: (layer * nb + b, 0, 0)),
                      one(COL_TILE),
                      pl.BlockSpec(e4.shape, lambda b, ix, pt: (0, 0)),
                      pl.BlockSpec(memory_space=pl.ANY)],
            out_specs=one(NSA_HEADS * HEAD_DIM),
            scratch_shapes=[pltpu.VMEM((NSA_KV_HEADS, (NSA_TOPN - 1) * NSA_BLOCK * SLABS, HEAD_DIM), F32),
                            pltpu.SemaphoreType.DMA((1,))]),
        out_shape=jax.ShapeDtypeStruct((nb, 1, NSA_HEADS * HEAD_DIM), F32),
        compiler_params=_params("arbitrary"),
        name="nsa_sample",
    )(idx_flat, table.reshape(-1), qslc, o_c, slc_new, win_new, ws, ng, e4, cache)


PAGES_PER_STEP = 16
SB_DEC_KB = 256


def _page_pipeline(pt_ref, cache_ref, buf, sem, layer, npg, n_chunks, reverse):
    b = pl.program_id(0)
    c = pl.program_id(1)
    step = b * n_chunks + c
    slot = step % 2
    rpp = cache_ref.shape[2]

    def copies(bb, cc, sl, start):
        first = (n_chunks - 1 - cc if reverse else cc) * PAGES_PER_STEP
        for i in range(PAGES_PER_STEP):
            page = pt_ref[bb * npg + first + i]
            cp = pltpu.make_async_copy(cache_ref.at[layer, page], buf.at[sl, pl.ds(i * rpp, rpp), :], sem.at[sl])
            if start:
                cp.start(priority=i % 2)
            else:
                cp.wait()

    @pl.when(step == 0)
    def _():
        copies(b, c, slot, True)

    @pl.when(step + 1 < pl.num_programs(0) * n_chunks)
    def _():
        nxt = step + 1
        copies(nxt // n_chunks, nxt % n_chunks, 1 - slot, True)

    copies(b, c, slot, False)
    return slot


def _sb_sample_kernel(pt_ref, q_ref, lw_ref, e_ref, cache_ref, out_ref, buf, sem, tail_ref, acc_ref,
                      *, layer, npg, n_chunks):
    c = pl.program_id(1)
    slot = _page_pipeline(pt_ref, cache_ref, buf, sem, layer, npg, n_chunks, reverse=True)
    kw = SB_HEADS * HEAD_DIM
    kb = SB_DEC_KB

    @pl.when(c == 0)
    def _():
        tail_ref[...] = jnp.zeros_like(tail_ref)
        acc_ref[...] = jnp.zeros_like(acc_ref)

    qbd = q_ref[0]
    slabs = 2 * SB_HEADS
    n_keys = buf.shape[1] // slabs
    for j in reversed(range(n_keys // kb)):
        keys = _slab_rows(buf.at[slot], j * kb * slabs, kb, slabs, range(SB_HEADS))
        vals = _slab_rows(buf.at[slot], j * kb * slabs, kb, slabs, range(SB_HEADS, slabs))
        z = _dot(keys.astype(MXU_DTYPE), qbd)
        sp = _softplus(z)
        hi, lo = _split(-sp)
        suffix = _dot(lw_ref[...], hi) + _dot(lw_ref[...], lo)
        tail = tail_ref[0:1, :]
        a = jnp.exp(z - sp + suffix[:kb] + tail)
        contrib = _dot(a.astype(MXU_DTYPE), e_ref[...]) * vals
        acc_ref[...] += jnp.sum(contrib.reshape(kb // 8, 8, kw), axis=0)
        tail_ref[...] = tail_ref[...] + suffix[kb:kb + 8]

    @pl.when(c == n_chunks - 1)
    def _():
        out_ref[0] = jnp.sum(acc_ref[...], axis=0, keepdims=True)


def _sb_sample(table, qbd, cache, layer):
    nb, npg = table.shape
    flat_rows = cache.shape[2]
    rpp = flat_rows // (2 * SB_HEADS)
    assert npg % PAGES_PER_STEP == 0 and (PAGES_PER_STEP * rpp) % SB_DEC_KB == 0
    n_chunks = npg // PAGES_PER_STEP
    lw = _suffix_matrix_rows(SB_DEC_KB)
    e = _expand_matrix(SB_HEADS, HEAD_DIM)
    kw = SB_HEADS * HEAD_DIM
    return pl.pallas_call(
        functools.partial(_sb_sample_kernel, layer=layer, npg=npg, n_chunks=n_chunks),
        grid_spec=pltpu.PrefetchScalarGridSpec(
            num_scalar_prefetch=1,
            grid=(nb, n_chunks),
            in_specs=[pl.BlockSpec((1,) + qbd.shape[1:], lambda b, c, pt: (b, 0, 0)),
                      pl.BlockSpec(lw.shape, lambda b, c, pt: (0, 0)),
                      pl.BlockSpec(e.shape, lambda b, c, pt: (0, 0)),
                      pl.BlockSpec(memory_space=pl.ANY)],
            out_specs=pl.BlockSpec((1, 1, kw), lambda b, c, pt: (b, 0, 0)),
            scratch_shapes=[pltpu.VMEM((2, PAGES_PER_STEP * flat_rows, HEAD_DIM), F32),
                            pltpu.SemaphoreType.DMA((2,)),
                            pltpu.VMEM((8, LANES), F32),
                            pltpu.VMEM((8, kw), F32)]),
        out_shape=jax.ShapeDtypeStruct((nb, 1, kw), F32),
        compiler_params=_params("arbitrary", "arbitrary"),
        name="sb_sample",
    )(table.reshape(-1), qbd, lw, e, cache)


DIFF_DEC_KB = 512


def _diff_sample_kernel(pt_ref, q_ref, e0_ref, e1_ref, new_ref, lam_ref, g_ref, cache_ref, out_ref,
                        buf, sem, m_ref, l_ref, acc0_ref, acc1_ref, *, layer, npg, n_chunks, lam_init):
    c = pl.program_id(1)
    slot = _page_pipeline(pt_ref, cache_ref, buf, sem, layer, npg, n_chunks, reverse=False)
    kw = DIFF_HEADS * HEAD_DIM
    kb = DIFF_DEC_KB

    @pl.when(c == 0)
    def _():
        m_ref[...] = jnp.full_like(m_ref, NEG_INF)
        l_ref[...] = jnp.zeros_like(l_ref)
        acc0_ref[...] = jnp.zeros_like(acc0_ref)
        acc1_ref[...] = jnp.zeros_like(acc1_ref)

    qbd = q_ref[0]

    def absorb(keys, v, vis):
        n = keys.shape[0]
        s = _dot(keys.astype(MXU_DTYPE), qbd)
        if vis is not None:
            s = jnp.where(vis, s, NEG_INF)
        m_old = m_ref[...]
        m_new = jnp.maximum(m_old, jnp.max(s, axis=0, keepdims=True))
        alpha = jnp.exp(m_old - m_new)
        p = jnp.exp(s - m_new)
        if vis is not None:
            p = jnp.where(vis, p, 0.0)
        l_ref[...] = alpha * l_ref[...] + jnp.sum(p, axis=0, keepdims=True)
        m_ref[...] = m_new
        pb = p.astype(MXU_DTYPE)
        a8 = jnp.concatenate([alpha] * 8, axis=0)
        for e_ref, acc_ref in ((e0_ref, acc0_ref), (e1_ref, acc1_ref)):
            pe = _dot(pb, e_ref[...])
            part = jnp.sum((pe * v).reshape(n // 8, 8, kw), axis=0)
            acc_ref[...] = acc_ref[...] * _dot2(a8, e_ref[...]) + part

    slabs = 2 * DIFF_HEADS
    n_keys = buf.shape[1] // slabs
    for j in range(n_keys // kb):
        absorb(_slab_rows(buf.at[slot], j * kb * slabs, kb, slabs, range(DIFF_HEADS)),
               _slab_rows(buf.at[slot], j * kb * slabs, kb, slabs, range(DIFF_HEADS, slabs)), None)

    @pl.when(c == n_chunks - 1)
    def _():
        new8 = jnp.concatenate([new_ref[0]] * 8, axis=0)
        first = lax.broadcasted_iota(jnp.int32, (8, LANES), 0) == 0
        absorb(new8[:, :kw], new8[:, kw:], first)
        l8 = jnp.concatenate([l_ref[...]] * 8, axis=0)
        o0 = jnp.sum(acc0_ref[...], axis=0, keepdims=True) / _dot2(l8, e0_ref[...])[0:1]
        o1 = jnp.sum(acc1_ref[...], axis=0, keepdims=True) / _dot2(l8, e1_ref[...])[0:1]
        lam = _diff_lambda(lam_ref, lam_init)
        for h in range(DIFF_HEADS):
            sl = slice(h * HEAD_DIM, (h + 1) * HEAD_DIM)
            out_ref[0, :, sl] = _diff_finish(o0[:, sl], o1[:, sl], lam, g_ref[...], lam_init)


def _diff_sample(table, qbd, new_rows, lam_rows, norm_g, cache, layer, lam_init):
    nb, npg = table.shape
    flat_rows = cache.shape[2]
    rpp = flat_rows // (2 * DIFF_HEADS)
    width = new_rows.shape[2]
    assert npg % PAGES_PER_STEP == 0 and (PAGES_PER_STEP * rpp) % DIFF_DEC_KB == 0
    n_chunks = npg // PAGES_PER_STEP
    kw = DIFF_HEADS * HEAD_DIM
    both = np.arange(LANES).reshape(LANES, 1)
    col_head = np.arange(kw).reshape(1, kw) // HEAD_DIM
    e0 = jnp.asarray(both == 2 * col_head, dtype=MXU_DTYPE)
    e1 = jnp.asarray(both == 2 * col_head + 1, dtype=MXU_DTYPE)
    return pl.pallas_call(
        functools.partial(_diff_sample_kernel, layer=layer, npg=npg, n_chunks=n_chunks, lam_init=lam_init),
        grid_spec=pltpu.PrefetchScalarGridSpec(
            num_scalar_prefetch=1,
            grid=(nb, n_chunks),
            in_specs=[pl.BlockSpec((1,) + qbd.shape[1:], lambda b, c, pt: (b, 0, 0)),
                      pl.BlockSpec(e0.shape, lambda b, c, pt: (0, 0)),
                      pl.BlockSpec(e1.shape, lambda b, c, pt: (0, 0)),
                      pl.BlockSpec((1, 1, width), lambda b, c, pt: (b, 0, 0)),
                      pl.BlockSpec(lam_rows.shape, lambda b, c, pt: (0, 0)),
                      pl.BlockSpec((1, HEAD_DIM), lambda b, c, pt: (0, 0)),
                      pl.BlockSpec(memory_space=pl.ANY)],
            out_specs=pl.BlockSpec((1, 1, kw), lambda b, c, pt: (b, 0, 0)),
            scratch_shapes=[pltpu.VMEM((2, PAGES_PER_STEP * flat_rows, HEAD_DIM), F32),
                            pltpu.SemaphoreType.DMA((2,)),
                            pltpu.VMEM((1, LANES), F32), pltpu.VMEM((1, LANES), F32),
                            pltpu.VMEM((8, kw), F32), pltpu.VMEM((8, kw), F32)]),
        out_shape=jax.ShapeDtypeStruct((nb, 1, kw), F32),
        compiler_params=_params("arbitrary", "arbitrary"),
        name="diff_sample",
    )(table.reshape(-1), qbd, e0, e1, new_rows, lam_rows, norm_g.reshape(1, HEAD_DIM), cache)


SAMPLE_ROWS = 16


def _pick_tm(m, cap):
    tm = min(m, cap)
    while m % tm:
        tm //= 2
    return tm


def kernel(x_prompt, x_sample, cache_nsa_cmp_kv, cache_nsa_slc_kv, state_nsa_win_kv, cache_sb_kv,
           cache_diff_kv, page_table, w_in, w_o, ln1_g, ln1_b, ln2_g, ln2_b, w_up, w_down,
           nsa_cmp_pos, nsa_cmp_w1, nsa_cmp_w2, diff_lambda, diff_norm_g):
    depth = w_in.shape[0]
    batch, seq, d_model = x_prompt.shape
    nbs, dec_seq, _ = x_sample.shape
    assert dec_seq == 1 and nbs <= SAMPLE_ROWS
    npg = page_table.shape[1]
    n_pool, page = cache_nsa_cmp_kv.shape[1], cache_nsa_cmp_kv.shape[2]
    past = npg * page
    alpha = (2 * depth) ** 0.25
    width = 2 * KV_WIDTH
    mp = batch * seq
    tm_p = _pick_tm(seq, 1024)

    tabs_p = (_rope_tables(jnp.arange(seq), HEAD_DIM, HEAD_DIM // ROT_FRACTION)
              + _rope_tables(jnp.arange(seq), DIFF_DIM, DIFF_DIM // ROT_FRACTION))
    pos_s = jnp.full((SAMPLE_ROWS,), past, jnp.int32)
    tabs_s = (_rope_tables(pos_s, HEAD_DIM, HEAD_DIM // ROT_FRACTION)
              + _rope_tables(pos_s, DIFF_DIM, DIFF_DIM // ROT_FRACTION))

    cmp_cache = cache_nsa_cmp_kv.reshape(depth, n_pool, page * SLABS, HEAD_DIM)
    slc_cache = cache_nsa_slc_kv.reshape(depth, n_pool, page * SLABS, HEAD_DIM)
    sb_cache = cache_sb_kv.reshape(depth, n_pool, page * 2 * SB_HEADS, HEAD_DIM)
    diff_cache = cache_diff_kv.reshape(depth, n_pool, page * 2 * DIFF_HEADS, HEAD_DIM)
    win_state = state_nsa_win_kv.reshape(depth, nbs, state_nsa_win_kv.shape[2] * SLABS, HEAD_DIM)
    prompt_pages = jnp.arange(mp // page, dtype=jnp.int32).reshape(1, mp // page)
    new_pages = jnp.arange(nbs, dtype=jnp.int32).reshape(1, nbs)
    w_t = _arrange_w_in(w_in)
    wo_b, wu_b, wd_b = w_o.astype(MXU_DTYPE), w_up.astype(MXU_DTYPE), w_down.astype(MXU_DTYPE)

    xp = x_prompt.reshape(mp, d_model)
    xs = jnp.pad(x_sample.reshape(nbs, d_model), ((0, SAMPLE_ROWS - nbs), (0, 0)))
    xp_b, xs_b = xp.astype(MXU_DTYPE), xs.astype(MXU_DTYPE)
    st_p = [[] for _ in range(5)]
    st_s = [[] for _ in range(5)]

    for l in range(depth):
        lam_init = 0.8 - 0.6 * math.exp(-0.3 * l)
        posc, w1bd, w2bd = _compress_weights(nsa_cmp_pos[l], nsa_cmp_w1[l], nsa_cmp_w2[l])
        lam_rows = diff_lambda[l].astype(F32)

        (qraw, qrot, cmp, slc, win, sbq, sbkv, dq, dkv, mg, ng) = _project(xp_b, w_t, l, tabs_p, tm_p, seq)
        kcv = _compress(cmp.reshape(1, mp // page, page * SLABS, HEAD_DIM), 0, prompt_pages, posc, w1bd, w2bd)
        o_nsa = _nsa_prompt(qraw, qrot, kcv, slc, win, ng, batch, seq)
        o_sb = _sb_prompt(sbq, sbkv, batch, seq)
        o_diff = _diff_prompt(dq, dkv, lam_rows, diff_norm_g[l], lam_init, batch, seq)
        x1, x1b = _merge(o_nsa, o_sb, o_diff, mg, xp, wo_b, l, ln1_g[l], ln1_b[l], alpha, _pick_tm(mp, 256))
        xp, xp_b = _ffn(x1, x1b, wu_b, wd_b, l, ln2_g[l], ln2_b[l], alpha, _pick_tm(mp, 512))
        win_rows = min(WINDOW, seq)
        st_p[0].append(cmp.reshape(batch, seq, 2, NSA_KV_HEADS, HEAD_DIM))
        st_p[1].append(slc.reshape(batch, seq, 2, NSA_KV_HEADS, HEAD_DIM))
        st_p[2].append(win.reshape(batch, seq, 2, NSA_KV_HEADS, HEAD_DIM)[:, seq - win_rows:])
        st_p[3].append(sbkv.reshape(batch, seq, 2, SB_HEADS, HEAD_DIM))
        st_p[4].append(dkv.reshape(batch, seq, 2, DIFF_HEADS, HEAD_DIM))

        (qraw, qrot, cmp, slc, win, sbq, sbkv, dq, dkv, mg, ng) = _project(
            xs_b, w_t, l, tabs_s, SAMPLE_ROWS, SAMPLE_ROWS)
        cmp, slc, win, sbkv, dkv = (a.reshape(SAMPLE_ROWS, -1) for a in (cmp, slc, win, sbkv, dkv))
        kcv_past = _compress(cmp_cache, l, page_table, posc, w1bd, w2bd)
        new_blk = jnp.zeros((1, nbs, NSA_BLOCK, width), F32).at[0, :, 0, :].set(cmp[:nbs])
        kcv_new = _compress(new_blk.reshape(1, nbs, NSA_BLOCK * SLABS, HEAD_DIM), 0, new_pages, posc, w1bd, w2bd)
        n_past = kcv_past.shape[1]
        n_rows = -(-(n_past + 1) // 8) * 8
        kcv_all = jnp.concatenate([kcv_past, kcv_new.reshape(nbs, 1, width),
                                   jnp.zeros((nbs, n_rows - n_past - 1, width), F32)], axis=1)
        scale = HEAD_DIM ** -0.5
        q4 = qraw[:nbs].reshape(nbs, NSA_KV_HEADS, NSA_GROUP, HEAD_DIM).transpose(0, 2, 1, 3)
        qsel = jnp.stack([_block_diag_q(q4[:, g].reshape(nbs, KV_WIDTH), NSA_KV_HEADS, HEAD_DIM, scale)
                          for g in range(NSA_GROUP)], axis=1)
        o_c, idx = _select(qsel, kcv_all, past)
        qr = qrot[:nbs].reshape(nbs, NSA_KV_HEADS, NSA_GROUP, HEAD_DIM).astype(F32) * scale
        qslc = jnp.pad(qr.transpose(0, 1, 3, 2), ((0, 0), (0, 0), (0, 0), (0, LANES - NSA_GROUP))).astype(MXU_DTYPE)
        o_nsa = _nsa_sample(idx, page_table, qslc, o_c, slc[:nbs].reshape(nbs, 1, width),
                            win[:nbs].reshape(nbs, 1, width), win_state, ng[:nbs].reshape(nbs, 1, COL_TILE),
                            slc_cache, l)
        o_sb = _sb_sample(page_table, _block_diag_q(sbq[:nbs], SB_HEADS, HEAD_DIM, scale), sb_cache, l)
        o_diff = _diff_sample(page_table, _block_diag_q(dq[:nbs], 2 * DIFF_HEADS, DIFF_DIM, DIFF_DIM ** -0.5),
                              dkv[:nbs].reshape(nbs, 1, -1), lam_rows, diff_norm_g[l], diff_cache, l, lam_init)

        def pad_rows(a):
            return jnp.pad(a.reshape(nbs, -1), ((0, SAMPLE_ROWS - nbs), (0, 0)))

        x1, x1b = _merge(pad_rows(o_nsa), pad_rows(o_sb), pad_rows(o_diff), mg, xs, wo_b, l, ln1_g[l], ln1_b[l],
                         alpha, SAMPLE_ROWS)
        xs, xs_b = _ffn(x1, x1b, wu_b, wd_b, l, ln2_g[l], ln2_b[l], alpha, SAMPLE_ROWS)
        new_win = win[:nbs].reshape(nbs, 1, 2, NSA_KV_HEADS, HEAD_DIM)
        wb = state_nsa_win_kv.shape[2]
        keep = min(WINDOW, past + 1)
        st_s[0].append(cmp[:nbs].reshape(nbs, 1, 2, NSA_KV_HEADS, HEAD_DIM))
        st_s[1].append(slc[:nbs].reshape(nbs, 1, 2, NSA_KV_HEADS, HEAD_DIM))
        st_s[2].append(jnp.concatenate([state_nsa_win_kv[l], new_win], axis=1)[:, wb + 1 - keep:])
        st_s[3].append(sbkv[:nbs].reshape(nbs, 1, 2, SB_HEADS, HEAD_DIM))
        st_s[4].append(dkv[:nbs].reshape(nbs, 1, 2, DIFF_HEADS, HEAD_DIM))

    return (xp.reshape(batch, seq, d_model), xs[:nbs].reshape(nbs, 1, d_model),
            jnp.stack(st_p[0]), jnp.stack(st_s[0]),
            jnp.stack(st_p[1]), jnp.stack(st_s[1]),
            jnp.stack(st_p[2]), jnp.stack(st_s[2]),
            jnp.stack(st_p[3]), jnp.stack(st_s[3]),
            jnp.stack(st_p[4]), jnp.stack(st_s[4]))
```
